```python
import math
import jax
import jax.numpy as jnp
from jax import lax
import numpy as np

D_MODEL = 1024
BATCH = 16
SEQ = 2048
DEPTH = 4

GROUP_HEADS = 4
HEAD_DIM = 64
GROUP_WIDTH = GROUP_HEADS * HEAD_DIM
N_GROUPS = 4
D_MIX = N_GROUPS * GROUP_WIDTH
IN_SIZES = (GROUP_WIDTH, GROUP_WIDTH, GROUP_WIDTH, GROUP_WIDTH, GROUP_HEADS, GROUP_HEADS, GROUP_WIDTH, GROUP_WIDTH, GROUP_WIDTH, GROUP_WIDTH, GROUP_WIDTH, GROUP_WIDTH, GROUP_WIDTH, GROUP_WIDTH, GROUP_WIDTH, GROUP_WIDTH)
N_IN = sum(IN_SIZES)
MLSTM_F_OFFSET = 4 * GROUP_WIDTH + GROUP_HEADS
MLSTM_CHUNK = 64
HGRN_CHUNK = 64
SCONV_WIDTH = 3
MOBA_BLOCK = 256
MOBA_TOPK = 3
MOBA_QCHUNK = 32
REL_BUCKETS = 32
REL_MAX_DIST = 128
N_MEM = 256
CROSS_HEADS = 4
CROSS_HEAD_DIM = 128
CROSS_WIDTH = CROSS_HEADS * CROSS_HEAD_DIM
D_FF = 2816
FFN_CONV_WIDTH = 3
RMS_EPS = 1e-6
NEG_BIG = -1e30

kernel_name = "hybrid_parallel_group_trunk"


def rmsnorm(x, gain):
    x32 = x.astype(jnp.float32)
    y = x32 * lax.rsqrt(jnp.mean(x32 * x32, axis=-1, keepdims=True) + RMS_EPS)
    return (y * gain.astype(jnp.float32)).astype(x.dtype)


def head_rmsnorm(h, gain):
    b, nh, s, d = h.shape
    h32 = h.astype(jnp.float32).transpose(0, 2, 1, 3)
    h32 = h32 * lax.rsqrt(jnp.mean(h32 * h32, axis=-1, keepdims=True) + RMS_EPS)
    return h32.reshape(b, s, nh * d) * gain.astype(jnp.float32)


def causal_dwconv(x, w):
    width, ch = w.shape
    return lax.conv_general_dilated(x, w.astype(x.dtype)[:, None, :], window_strides=(1,), padding=[(width - 1, 0)], dimension_numbers=('NWC', 'WIO', 'NWC'), feature_group_count=ch)


def to_heads(a, n_heads):
    b, s, _ = a.shape
    return a.reshape(b, s, n_heads, -1).transpose(0, 2, 1, 3)


def _chunks(a, size):
    b, h, s = a.shape[:3]
    return jnp.moveaxis(a.reshape(b, h, s // size, size, *a.shape[3:]), 2, 0)


def _unchunk(a):
    nc, b, h, l, d = a.shape
    return jnp.moveaxis(a, 0, 2).reshape(b, h, nc * l, d)


def rel_bucket(dist):
    n = jnp.maximum(dist, 0)
    exact = REL_BUCKETS // 2
    nf = jnp.maximum(n, 1).astype(jnp.float32)
    large = exact + (jnp.log(nf / exact) / math.log(REL_MAX_DIST / exact) * (REL_BUCKETS - exact)).astype(jnp.int32)
    large = jnp.minimum(large, REL_BUCKETS - 1)
    return jnp.where(n < exact, n, large)


def mlstm_chunkwise(q, k, v, i_pre, f_pre):
    b, nh, s, d = q.shape
    f32 = jnp.float32
    q = q.astype(f32)
    k = k.astype(f32) * (d ** -0.5)
    v = v.astype(f32)
    log_i = i_pre.astype(f32)
    log_f = jax.nn.log_sigmoid(f_pre.astype(f32))
    causal = jnp.tril(jnp.ones((MLSTM_CHUNK, MLSTM_CHUNK), dtype=bool))

    def step(carry, xs):
        c_mat, n_vec, m_prev = carry
        qc, kc, vc, lic, lfc = xs
        g = jnp.cumsum(lfc, axis=-1)
        dlog = jnp.where(causal, g[..., :, None] - g[..., None, :] + lic[..., None, :], NEG_BIG)
        a = g + m_prev[..., None]
        m_t = jnp.maximum(a, jnp.max(dlog, axis=-1))
        w = jnp.exp(dlog - m_t[..., None]) * jnp.einsum('bhtd,bhsd->bhts', qc, kc)
        inter = jnp.exp(a - m_t)
        num = jnp.einsum('bhts,bhsd->bhtd', w, vc) + inter[..., None] * jnp.einsum('bhvk,bhtk->bhtv', c_mat, qc)
        den = jnp.sum(w, axis=-1) + inter * jnp.einsum('bhk,bhtk->bht', n_vec, qc)
        h = num / jnp.maximum(jnp.abs(den), jnp.exp(-m_t))[..., None]
        g_last = g[..., -1]
        u = g_last[..., None] - g + lic
        m_new = jnp.maximum(g_last + m_prev, jnp.max(u, axis=-1))
        ws = jnp.exp(u - m_new[..., None])
        decay = jnp.exp(g_last + m_prev - m_new)
        c_new = decay[..., None, None] * c_mat + jnp.einsum('bhs,bhsv,bhsk->bhvk', ws, vc, kc)
        n_new = decay[..., None] * n_vec + jnp.einsum('bhs,bhsk->bhk', ws, kc)
        return (c_new, n_new, m_new), h

    init = (jnp.zeros((b, nh, d, d), f32), jnp.zeros((b, nh, d), f32), jnp.zeros((b, nh), f32))
    _, hs = lax.scan(step, init, tuple(_chunks(a, MLSTM_CHUNK) for a in (q, k, v, log_i, log_f)))
    return _unchunk(hs)


def hgrn2_chunkwise(q, k, log_f, v):
    b, nh, s, dk = q.shape
    dv = v.shape[-1]
    causal = jnp.tril(jnp.ones((HGRN_CHUNK, HGRN_CHUNK), dtype=bool))[:, :, None]

    def step(state, xs):
        qc, kc, lfc, vc = xs
        g = jnp.cumsum(lfc, axis=2)
        rel = jnp.exp(jnp.where(causal, g[:, :, :, None, :] - g[:, :, None, :, :], NEG_BIG))
        attn = jnp.einsum('bhtk,bhtsk,bhsk->bhts', qc, rel, kc)
        out = jnp.einsum('bhts,bhsv->bhtv', attn, vc) + jnp.einsum('bhtk,bhkv->bhtv', qc * jnp.exp(g), state)
        g_last = g[:, :, -1:, :]
        new_state = jnp.exp(g_last[:, :, 0, :])[..., None] * state + jnp.einsum('bhsk,bhsv->bhkv', kc * jnp.exp(g_last - g), vc)
        return new_state, out

    init = jnp.zeros((b, nh, dk, dv), jnp.float32)
    _, outs = lax.scan(step, init, tuple(_chunks(a, HGRN_CHUNK) for a in (q, k, log_f, v)))
    return _unchunk(outs)


def moba_attention(q, k, v, rel_bias):
    b, nh, s, d = q.shape
    f32 = jnp.float32
    nb = -(-s // MOBA_BLOCK)
    s_pad = nb * MOBA_BLOCK
    padw = ((0, 0), (0, 0), (0, s_pad - s), (0, 0))
    q, k, v = jnp.pad(q, padw), jnp.pad(k, padw), jnp.pad(v, padw)
    kb = k.reshape(b, nh, nb, MOBA_BLOCK, d)
    vb = v.reshape(b, nh, nb, MOBA_BLOCK, d)
    n_sel = min(MOBA_TOPK, nb - 1)
    scale = d ** -0.5
    bias_t = rel_bias.astype(f32).T
    blk_of = jnp.arange(s_pad, dtype=jnp.int32) // MOBA_BLOCK
    if n_sel > 0:
        kmean = jnp.mean(kb.astype(f32), axis=3)
        gate = jnp.einsum('bhsd,bhnd->bhsn', q.astype(f32), kmean)
        past = jnp.arange(nb, dtype=jnp.int32)[None, :] < blk_of[:, None]
        gate = jnp.where(past, gate, NEG_BIG)
        _, sel = lax.top_k(gate, n_sel)
        sel = sel.astype(jnp.int32)
        sel_valid = sel < blk_of[:, None]
    b_ix = jnp.arange(b)[:, None, None, None]
    h_ix = jnp.arange(nh)[None, :, None, None]
    key_off = jnp.arange(MOBA_BLOCK, dtype=jnp.int32)

    def attend_chunk(c):
        start = c * MOBA_QCHUNK
        qc = lax.dynamic_slice_in_dim(q, start, MOBA_QCHUNK, axis=2)
        t = start + jnp.arange(MOBA_QCHUNK, dtype=jnp.int32)
        j = start // MOBA_BLOCK
        k_own = lax.dynamic_index_in_dim(kb, j, axis=2, keepdims=False)
        v_own = lax.dynamic_index_in_dim(vb, j, axis=2, keepdims=False)
        dist_own = t[:, None] - (j * MOBA_BLOCK + key_off)[None, :]
        s_own = jnp.einsum('bhqd,bhkd->bhqk', qc, k_own).astype(f32) * scale + bias_t[:, rel_bucket(dist_own)][None]
        s_own = jnp.where(dist_own >= 0, s_own, NEG_BIG)
        if n_sel == 0:
            p = jax.nn.softmax(s_own, axis=-1)
            out = jnp.einsum('bhqk,bhkd->bhqd', p, v_own.astype(f32))
        else:
            sel_c = lax.dynamic_slice_in_dim(sel, start, MOBA_QCHUNK, axis=2)
            valid_c = lax.dynamic_slice_in_dim(sel_valid, start, MOBA_QCHUNK, axis=2)
            k_sel = kb[b_ix, h_ix, sel_c]
            v_sel = vb[b_ix, h_ix, sel_c]
            dist = t[None, None, :, None, None] - (sel_c[..., None] * MOBA_BLOCK + key_off)
            bias = bias_t[h_ix[..., None], rel_bucket(dist)]
            s_past = jnp.einsum('bhqd,bhqnkd->bhqnk', qc, k_sel).astype(f32) * scale + bias
            s_past = jnp.where(valid_c[..., None], s_past, NEG_BIG)
            n_past = n_sel * MOBA_BLOCK
            p = jax.nn.softmax(jnp.concatenate([s_past.reshape(b, nh, MOBA_QCHUNK, n_past), s_own], axis=-1), axis=-1)
            p_past = p[..., :n_past].reshape(b, nh, MOBA_QCHUNK, n_sel, MOBA_BLOCK)
            out = jnp.einsum('bhqnk,bhqnkd->bhqd', p_past, v_sel.astype(f32)) + jnp.einsum('bhqk,bhkd->bhqd', p[..., n_past:], v_own.astype(f32))
        return out.astype(q.dtype)

    outs = lax.map(attend_chunk, jnp.arange(s_pad // MOBA_QCHUNK, dtype=jnp.int32))
    return _unchunk(outs)[:, :, :s]


def token_mixer(xn, w_in, b_in, mlstm_norm, sconv_w, rel_bias, hgrn_lb, hgrn_norm, w_out):
    b, s, _ = xn.shape
    f32 = jnp.float32
    dt = xn.dtype
    proj = xn @ w_in + b_in
    splits = np.cumsum(IN_SIZES)[:-1].tolist()
    (m_q, m_k, m_v, m_o, m_i, m_f, c_b, c_c, c_h, a_q, a_k, a_v, h_q, h_f, h_i, h_g) = jnp.split(proj, splits, axis=-1)
    h_m = mlstm_chunkwise(to_heads(m_q, GROUP_HEADS), to_heads(m_k, GROUP_HEADS), to_heads(m_v, GROUP_HEADS), m_i.transpose(0, 2, 1), m_f.transpose(0, 2, 1))
    y_m = head_rmsnorm(h_m, mlstm_norm) * jax.nn.sigmoid(m_o.astype(f32))
    y_c = c_b * causal_dwconv(c_c * c_h, sconv_w)
    y_a = moba_attention(to_heads(a_q, GROUP_HEADS), to_heads(a_k, GROUP_HEADS), to_heads(a_v, GROUP_HEADS), rel_bias)
    y_a = y_a.transpose(0, 2, 1, 3).reshape(b, s, GROUP_WIDTH)
    lb = hgrn_lb.astype(f32).reshape(1, GROUP_HEADS, 1, HEAD_DIM)
    z = to_heads(h_f, GROUP_HEADS).astype(f32)
    f_gate = lb + (1.0 - lb) * jax.nn.sigmoid(z)
    log_f = jnp.log(f_gate)
    k_h = 1.0 - f_gate
    q_h = jax.nn.silu(to_heads(h_q, GROUP_HEADS).astype(f32))
    o_h = hgrn2_chunkwise(q_h, k_h, log_f, to_heads(h_i, GROUP_HEADS).astype(f32))
    y_h = head_rmsnorm(o_h, hgrn_norm) * jax.nn.silu(h_g.astype(f32))
    y = jnp.concatenate([y_m.astype(dt), y_c.astype(dt), y_a.astype(dt), y_h.astype(dt)], axis=-1)
    return y @ w_out


def memory_cross_attention(xn, memn, wq, wk, wv, wo):
    b, s, _ = xn.shape
    m = memn.shape[1]
    q = (xn @ wq).reshape(b, s, CROSS_HEADS, CROSS_HEAD_DIM)
    k = (memn @ wk).reshape(b, m, CROSS_HEADS, CROSS_HEAD_DIM)
    v = (memn @ wv).reshape(b, m, CROSS_HEADS, CROSS_HEAD_DIM)
    logits = jnp.einsum('bshd,bmhd->bhsm', q, k).astype(jnp.float32) * (CROSS_HEAD_DIM ** -0.5)
    p = jax.nn.softmax(logits, axis=-1)
    o = jnp.einsum('bhsm,bmhd->bshd', p, v.astype(jnp.float32)).astype(xn.dtype).reshape(b, s, CROSS_WIDTH)
    return o @ wo


def conv_glu_ffn(xn, w_up, conv_w, conv_b, w_down):
    gu = xn @ w_up
    gate, up = jnp.split(gu, 2, axis=-1)
    gate = causal_dwconv(gate, conv_w) + conv_b.astype(gate.dtype)
    return (jax.nn.silu(gate) * up) @ w_down


def setup_inputs(seed: int = 0) -> dict:
    key = jax.random.key(seed)
    ks = jax.random.split(key, 26)
    f32 = jnp.float32

    def nrm(k, shape, scale):
        return scale * jax.random.normal(k, shape, f32)

    def gain(k, shape):
        return 1.0 + 0.05 * jax.random.normal(k, shape, f32)

    b_in = nrm(ks[3], (DEPTH, N_IN), 0.01)
    b_in = b_in.at[:, MLSTM_F_OFFSET:MLSTM_F_OFFSET + GROUP_HEADS].add(jnp.linspace(3.0, 6.0, GROUP_HEADS))
    return {
        'x': nrm(ks[0], (BATCH, SEQ, D_MODEL), 1.0),
        'mem': nrm(ks[1], (BATCH, N_MEM, D_MODEL), 1.0),
        'w_in': nrm(ks[2], (DEPTH, D_MODEL, N_IN), D_MODEL ** -0.5),
        'b_in': b_in,
        'mlstm_norm': gain(ks[4], (DEPTH, GROUP_WIDTH)),
        'sconv_w': nrm(ks[5], (DEPTH, SCONV_WIDTH, GROUP_WIDTH), SCONV_WIDTH ** -0.5),
        'rel_bias': nrm(ks[6], (REL_BUCKETS, GROUP_HEADS), 0.5),
        'hgrn_lb_logits': nrm(ks[7], (DEPTH, GROUP_WIDTH), 0.5),
        'hgrn_norm': gain(ks[8], (DEPTH, GROUP_WIDTH)),
        'w_mix_out': nrm(ks[9], (DEPTH, D_MIX, D_MODEL), D_MIX ** -0.5),
        'norm_mix_pre': gain(ks[10], (DEPTH, D_MODEL)),
        'norm_mix_post': gain(ks[11], (DEPTH, D_MODEL)),
        'mem_norm': gain(ks[12], (DEPTH, D_MODEL)),
        'w_cq': nrm(ks[13], (DEPTH, D_MODEL, CROSS_WIDTH), D_MODEL ** -0.5),
        'w_ck': nrm(ks[14], (DEPTH, D_MODEL, CROSS_WIDTH), D_MODEL ** -0.5),
        'w_cv': nrm(ks[15], (DEPTH, D_MODEL, CROSS_WIDTH), D_MODEL ** -0.5),
        'w_co': nrm(ks[16], (DEPTH, CROSS_WIDTH, D_MODEL), CROSS_WIDTH ** -0.5),
        'norm_cross_pre': gain(ks[17], (DEPTH, D_MODEL)),
        'norm_cross_post': gain(ks[18], (DEPTH, D_MODEL)),
        'w_ffn_in': nrm(ks[19], (DEPTH, D_MODEL, 2 * D_FF), D_MODEL ** -0.5),
        'ffn_conv_w': nrm(ks[20], (DEPTH, FFN_CONV_WIDTH, D_FF), FFN_CONV_WIDTH ** -0.5),
        'ffn_conv_b': nrm(ks[21], (DEPTH, D_FF), 0.01),
        'w_ffn_out': nrm(ks[22], (DEPTH, D_FF, D_MODEL), D_FF ** -0.5),
        'norm_ffn_pre': gain(ks[23], (DEPTH, D_MODEL)),
        'norm_ffn_post': gain(ks[24], (DEPTH, D_MODEL)),
    }


def reference(x, mem, w_in, b_in, mlstm_norm, sconv_w, rel_bias, hgrn_lb_logits, hgrn_norm, w_mix_out, norm_mix_pre, norm_mix_post, mem_norm, w_cq, w_ck, w_cv, w_co, norm_cross_pre, norm_cross_post, w_ffn_in, ffn_conv_w, ffn_conv_b, w_ffn_out, norm_ffn_pre, norm_ffn_post):
    lb_soft = jax.nn.softmax(hgrn_lb_logits.astype(jnp.float32), axis=0)
    lb_all = jnp.cumsum(lb_soft, axis=0) - lb_soft[0]
    for l in range(DEPTH):
        h = token_mixer(rmsnorm(x, norm_mix_pre[l]), w_in[l], b_in[l], mlstm_norm[l], sconv_w[l], rel_bias, lb_all[l], hgrn_norm[l], w_mix_out[l])
        x = x + rmsnorm(h, norm_mix_post[l])
        h = memory_cross_attention(rmsnorm(x, norm_cross_pre[l]), rmsnorm(mem, mem_norm[l]), w_cq[l], w_ck[l], w_cv[l], w_co[l])
        x = x + rmsnorm(h, norm_cross_post[l])
        h = conv_glu_ffn(rmsnorm(x, norm_ffn_pre[l]), w_ffn_in[l], ffn_conv_w[l], ffn_conv_b[l], w_ffn_out[l])
        x = x + rmsnorm(h, norm_ffn_post[l])
    return x
```

```python
import functools
import math

import numpy as np
import jax
import jax.numpy as jnp
from jax import lax
from jax.experimental import pallas as pl
from jax.experimental.pallas import tpu as pltpu

F32 = jnp.float32
BF16 = jnp.bfloat16

HEADS = 4
HEAD_DIM = 64
GW = HEADS * HEAD_DIM
MOBA_BLOCK = 256
MOBA_TOPK = 3
REL_BUCKETS = 32
REL_MAX_DIST = 128
CROSS_HEADS = 4
CROSS_HEAD_DIM = 128
RMS_EPS = 1e-6
NEG_BIG = -1e30

MLSTM_CHUNK = 256
HGRN_CHUNK = 64
VMEM_LIMIT = 56 * 1024 * 1024

(CB_MQ, CB_MK, CB_MV, CB_MO, CB_CB, CB_CC, CB_CH, CB_AQ, CB_AK, CB_AV,
 CB_HQ, CB_HI, CB_HG) = range(13)
N_MAIN = 13 * GW
N_AUX = 512


def _rms(x, g):
    return x * lax.rsqrt(jnp.mean(x * x, axis=-1, keepdims=True) + RMS_EPS) * g


def _nt(a, b):
    return lax.dot_general(a, b, (((1,), (1,)), ((), ())), preferred_element_type=F32)


def _tn(a, b):
    return lax.dot_general(a, b, (((0,), (0,)), ((), ())), preferred_element_type=F32)


def _dot(a, b):
    return jnp.dot(a, b, preferred_element_type=F32)


def _head_of(shape, axis):
    return lax.shift_right_logical(lax.broadcasted_iota(jnp.int32, shape, axis), 6)


def _expand_heads(cols, width=GW):
    rows = cols[0].shape[0]
    hid = _head_of((rows, width), 1)
    return jnp.where(hid == 0, cols[0], jnp.where(hid == 1, cols[1], jnp.where(hid == 2, cols[2], cols[3])))


def _params(*sem):
    return pltpu.CompilerParams(dimension_semantics=sem, vmem_limit_bytes=VMEM_LIMIT)


def _inproj_body(x_ref, g_ref, wm_ref, bm_ref, wa_ref, ba_ref, om_ref, oa_ref):
    xn = _rms(x_ref[...], g_ref[...]).astype(BF16)
    for c in range(0, om_ref.shape[-1], GW):
        acc = _dot(xn, wm_ref[:, c:c + GW]) + bm_ref[:, c:c + GW]
        om_ref[:, c:c + GW] = acc.astype(om_ref.dtype)
    oa_ref[...] = _dot(xn, wa_ref[...]) + ba_ref[...]


def _inproj(x2, gain, wm, bm, wa, ba, tm):
    m, d = x2.shape
    const = lambda i: (0, 0)
    return pl.pallas_call(
        _inproj_body,
        grid=(m // tm,),
        in_specs=[
            pl.BlockSpec((tm, d), lambda i: (i, 0)),
            pl.BlockSpec((1, d), const),
            pl.BlockSpec((d, N_MAIN), const),
            pl.BlockSpec((1, N_MAIN), const),
            pl.BlockSpec((d, N_AUX), const),
            pl.BlockSpec((1, N_AUX), const),
        ],
        out_specs=[
            pl.BlockSpec((tm, N_MAIN), lambda i: (i, 0)),
            pl.BlockSpec((tm, N_AUX), lambda i: (i, 0)),
        ],
        out_shape=[
            jax.ShapeDtypeStruct((m, N_MAIN), BF16),
            jax.ShapeDtypeStruct((m, N_AUX), F32),
        ],
        compiler_params=_params("parallel"),
        name="inproj",
    )(x2, gain, wm, bm, wa, ba)


def _sconv_body(b_ref, c_ref, h_ref, w_ref, o_ref):
    u = c_ref[0].astype(F32) * h_ref[0].astype(F32)
    row = lax.broadcasted_iota(jnp.int32, u.shape, 0)
    u1 = jnp.where(row >= 1, pltpu.roll(u, 1, 0), 0.0)
    u2 = jnp.where(row >= 2, pltpu.roll(u, 2, 0), 0.0)
    w = w_ref[...]
    y = w[0:1] * u2 + w[1:2] * u1 + w[2:3] * u
    o_ref[0] = (b_ref[0].astype(F32) * y).astype(o_ref.dtype)


def _sconv(main3, w):
    b, s, _ = main3.shape
    col = lambda cb: pl.BlockSpec((1, s, GW), lambda i: (i, 0, cb))
    return pl.pallas_call(
        _sconv_body,
        grid=(b,),
        in_specs=[col(CB_CB), col(CB_CC), col(CB_CH), pl.BlockSpec(w.shape, lambda i: (0, 0))],
        out_specs=pl.BlockSpec((1, s, GW), lambda i: (i, 0, 0)),
        out_shape=jax.ShapeDtypeStruct((b, s, GW), BF16),
        compiler_params=_params("parallel"),
        name="sconv",
    )(main3, main3, main3, w)


def _scan_lanes(x, op):
    n = x.shape[1]
    lane = lax.broadcasted_iota(jnp.int32, x.shape, 1)
    s = 1
    while s < n:
        x = jnp.where(lane >= s, op(x, pltpu.roll(x, s, 1)), x)
        s *= 2
    return x


def _head_rms_gate(hval, gain, gate, bones):
    sq = hval * hval
    hi = sq.astype(BF16)
    lo = (sq - hi.astype(F32)).astype(BF16)
    ms = (_dot(hi, bones) + _dot(lo, bones)) * (1.0 / HEAD_DIM)
    return hval * lax.rsqrt(ms + RMS_EPS) * gain * gate


def _block_ones():
    r = _head_of((GW, GW), 0)
    c = _head_of((GW, GW), 1)
    return r == c


def _mlstm_body(q_ref, k_ref, v_ref, o_ref, g_ref, gain_ref, y_ref, ct_ref, n_ref, m_ref):
    s_len = q_ref.shape[1]
    L = MLSTM_CHUNK
    bd = _block_ones()
    bones = jnp.where(bd, 1.0, 0.0).astype(BF16)
    hid_l = _head_of((L, GW), 1)
    hmask = [hid_l == h for h in range(HEADS)]
    tril = lax.broadcasted_iota(jnp.int32, (L, L), 0) >= lax.broadcasted_iota(jnp.int32, (L, L), 1)
    row8 = lax.broadcasted_iota(jnp.int32, (8, L), 0)
    nrow = lax.broadcasted_iota(jnp.int32, (8, GW), 0)
    nkeep = nrow == _head_of((8, GW), 1)
    gain = gain_ref[...]

    ct_ref[...] = jnp.zeros_like(ct_ref)
    n_ref[...] = jnp.zeros_like(n_ref)
    m_ref[...] = jnp.zeros_like(m_ref)

    def step(c, carry):
        r0 = pl.multiple_of(c * L, L)
        q = q_ref[0, pl.ds(r0, L), :]
        k = (k_ref[0, pl.ds(r0, L), :].astype(F32) * (HEAD_DIM ** -0.5)).astype(BF16)
        v = v_ref[0, pl.ds(r0, L), :]
        gt = jnp.transpose(g_ref[0, pl.ds(r0, L), :])[0:8]
        li = gt
        lf = jax.nn.log_sigmoid(pltpu.roll(gt, 4, 0))
        g = _scan_lanes(lf, jnp.add)
        cc = li - g
        mprev = m_ref[...][:, 0:1]
        pp = jnp.maximum(mprev, _scan_lanes(cc, jnp.maximum))
        ein = jnp.exp(mprev - pp)
        p_last = pp[:, L - 1:L]
        ws = jnp.exp(cc - p_last)
        decay = jnp.exp(mprev - p_last)
        m_new = g[:, L - 1:L] + p_last

        b0 = jnp.where(row8 < 4, pp, pltpu.roll(ein, 4, 0))
        b1 = jnp.where(row8 < 4, g + pp, pltpu.roll(ws, 4, 0))
        cols = jnp.transpose(jnp.concatenate([b0, b1, jnp.zeros((112, L), F32)], axis=0))

        ct = ct_ref[...]
        qc = _nt(q, ct.astype(BF16))
        qn = _nt(q, n_ref[...].astype(BF16))
        num = jnp.zeros((L, GW), F32)
        inv_cols = []
        for h in range(HEADS):
            qh = jnp.where(hmask[h], q, jnp.zeros_like(q))
            sc = _nt(qh, k)
            dlog = jnp.where(tril, cc[h:h + 1, :] - cols[:, h:h + 1], NEG_BIG)
            w = jnp.exp(dlog) * sc
            den = jnp.sum(w, axis=-1, keepdims=True) + cols[:, 4 + h:5 + h] * qn[:, h:h + 1]
            inv_cols.append(1.0 / jnp.maximum(jnp.abs(den), jnp.exp(-cols[:, 8 + h:9 + h])))
            num = num + jnp.where(hmask[h], _dot(w.astype(BF16), v), 0.0)
        num = num + _expand_heads([cols[:, 4 + h:5 + h] for h in range(HEADS)]) * qc
        hval = num * _expand_heads(inv_cols)
        gate = jax.nn.sigmoid(o_ref[0, pl.ds(r0, L), :].astype(F32))
        y_ref[0, pl.ds(r0, L), :] = _head_rms_gate(hval, gain, gate, bones).astype(y_ref.dtype)

        vw = (v.astype(F32) * _expand_heads([cols[:, 12 + h:13 + h] for h in range(HEADS)])).astype(BF16)
        hid1 = _head_of((1, GW), 1)
        dec_l = jnp.where(hid1 == 0, decay[0:1], jnp.where(hid1 == 1, decay[1:2], jnp.where(hid1 == 2, decay[2:3], decay[3:4])))
        ct_ref[...] = ct * dec_l + jnp.where(bd, _tn(vw, k), 0.0)
        n_ref[...] = n_ref[...] * decay + jnp.where(nkeep, _dot(ws.astype(BF16), k), 0.0)
        m_ref[...] = jnp.broadcast_to(m_new, m_ref.shape)
        return carry

    lax.fori_loop(0, s_len // L, step, 0)


def _mlstm(main3, aux3, gain):
    b, s, _ = main3.shape
    col = lambda cb: pl.BlockSpec((1, s, GW), lambda i: (i, 0, cb))
    return pl.pallas_call(
        _mlstm_body,
        grid=(b,),
        in_specs=[col(CB_MQ), col(CB_MK), col(CB_MV), col(CB_MO),
                  pl.BlockSpec((1, s, 128), lambda i: (i, 0, 2)),
                  pl.BlockSpec((1, GW), lambda i: (0, 0))],
        out_specs=pl.BlockSpec((1, s, GW), lambda i: (i, 0, 0)),
        out_shape=jax.ShapeDtypeStruct((b, s, GW), BF16),
        scratch_shapes=[pltpu.VMEM((GW, GW), F32), pltpu.VMEM((8, GW), F32), pltpu.VMEM((8, 128), F32)],
        compiler_params=_params("parallel"),
        name="mlstm",
    )(main3, main3, main3, main3, aux3, gain)


def _seg_prefix(x, seg, pos):
    s = 1
    while s < seg:
        x = x + jnp.where(pos >= s, pltpu.roll(x, s, 0), 0.0)
        s *= 2
    return x


def _seg_suffix(x, seg, pos):
    n = x.shape[0]
    s = 1
    while s < seg:
        x = x + jnp.where(pos + s < seg, pltpu.roll(x, n - s, 0), 0.0)
        s *= 2
    return x


def _hgrn_body(lbl_ref, q_ref, i_ref, g_ref, f_ref, gain_ref, y_ref, st_ref, *, layer):
    s_len = q_ref.shape[1]
    C = HGRN_CHUNK
    bd = _block_ones()
    bones = jnp.where(bd, 1.0, 0.0).astype(BF16)
    hid_c = _head_of((C, GW), 1)
    hmask = [hid_c == h for h in range(HEADS)]
    gain = gain_ref[...]

    lg = lbl_ref[...]
    ex = jnp.exp(lg - jnp.max(lg, axis=0, keepdims=True))
    soft = ex / jnp.sum(ex, axis=0, keepdims=True)
    lb = jnp.zeros((1, GW), F32)
    for j in range(1, layer + 1):
        lb = lb + soft[j:j + 1]

    tq = lax.broadcasted_iota(jnp.int32, (HEADS * C, C), 0) & (C - 1)
    tk = lax.broadcasted_iota(jnp.int32, (HEADS * C, C), 1)
    levels = []
    bsz = C
    while bsz >= 2:
        half = bsz // 2
        same = (tq // bsz) == (tk // bsz)
        levels.append((bsz, same & ((tq % bsz) >= half) & ((tk % bsz) < half)))
        bsz = half
    eye = tq == tk
    rowpos = lax.broadcasted_iota(jnp.int32, (C, GW), 0)

    st_ref[...] = jnp.zeros_like(st_ref)

    def stack_heads(a):
        return jnp.concatenate([jnp.where(hmask[h], a, 0.0) for h in range(HEADS)], axis=0).astype(BF16)

    def step(c, carry):
        r0 = pl.multiple_of(c * C, C)
        z = f_ref[0, pl.ds(r0, C), :]
        f = lb + (1.0 - lb) * jax.nn.sigmoid(z)
        lf = jnp.log(f)
        kk = 1.0 - f
        qq = jax.nn.silu(q_ref[0, pl.ds(r0, C), :].astype(F32))
        vv = i_ref[0, pl.ds(r0, C), :]

        gcum = _seg_prefix(lf, C, rowpos)
        st = st_ref[...]
        out = _nt((qq * jnp.exp(gcum)).astype(BF16), st.astype(BF16))

        att = jnp.where(eye, _nt(stack_heads(qq), kk.astype(BF16)), 0.0)
        for bsz, msk in levels:
            half = bsz // 2
            pos = rowpos & (half - 1)
            qd = qq * jnp.exp(_seg_prefix(lf, half, pos))
            kd = kk * jnp.exp(_seg_suffix(lf, half, pos) - lf)
            att = att + jnp.where(msk, _nt(stack_heads(qd), kd.astype(BF16)), 0.0)
        for h in range(HEADS):
            out = out + jnp.where(hmask[h], _dot(att[h * C:(h + 1) * C].astype(BF16), vv), 0.0)

        gate = jax.nn.silu(g_ref[0, pl.ds(r0, C), :].astype(F32))
        y_ref[0, pl.ds(r0, C), :] = _head_rms_gate(out, gain, gate, bones).astype(y_ref.dtype)

        g_last = gcum[C - 1:C, :]
        kdec = (kk * jnp.exp(g_last - gcum)).astype(BF16)
        st_ref[...] = st * jnp.exp(g_last) + jnp.where(bd, _tn(vv, kdec), 0.0)
        return carry

    lax.fori_loop(0, s_len // C, step, 0)


def _hgrn(main3, aux3, lb_logits, gain, layer):
    b, s, _ = main3.shape
    col = lambda cb: pl.BlockSpec((1, s, GW), lambda i: (i, 0, cb))
    return pl.pallas_call(
        functools.partial(_hgrn_body, layer=layer),
        grid=(b,),
        in_specs=[pl.BlockSpec(lb_logits.shape, lambda i: (0, 0)),
                  col(CB_HQ), col(CB_HI), col(CB_HG),
                  pl.BlockSpec((1, s, GW), lambda i: (i, 0, 0)),
                  pl.BlockSpec((1, GW), lambda i: (0, 0))],
        out_specs=pl.BlockSpec((1, s, GW), lambda i: (i, 0, 0)),
        out_shape=jax.ShapeDtypeStruct((b, s, GW), BF16),
        scratch_shapes=[pltpu.VMEM((GW, GW), F32)],
        compiler_params=_params("parallel"),
        name="hgrn2",
    )(lb_logits, main3, main3, main3, aux3, gain)


def _bucket_tables():
    def bucket(dist):
        n = np.maximum(dist, 0)
        exact = REL_BUCKETS // 2
        nf = np.maximum(n, 1).astype(np.float32)
        large = exact + (np.log(nf / np.float32(exact)) / np.float32(math.log(REL_MAX_DIST / exact))
                         * np.float32(REL_BUCKETS - exact)).astype(np.int32)
        large = np.minimum(large, REL_BUCKETS - 1)
        return np.where(n < exact, n, large).astype(np.int32)
    tk = np.arange(MOBA_BLOCK)[:, None]
    tq = np.arange(MOBA_BLOCK)[None, :]
    own = np.where(tq - tk >= 0, bucket(tq - tk), REL_BUCKETS)
    adj = bucket(MOBA_BLOCK + tq - tk)
    far = int(bucket(np.array([2 * MOBA_BLOCK]))[0])
    return np.stack([own, adj]).astype(np.int32), far


def _bias_body(rb_ref, bk_ref, o_ref):
    bk = bk_ref[...]
    for h in range(HEADS):
        acc = jnp.full(bk.shape, NEG_BIG, F32)
        for j in range(REL_BUCKETS):
            acc = jnp.where(bk == j, rb_ref[j, h], acc)
        o_ref[h] = acc


def _bias_tiles(rel_bias, buckets):
    return pl.pallas_call(
        _bias_body,
        in_specs=[pl.BlockSpec(memory_space=pltpu.SMEM), pl.BlockSpec(buckets.shape, lambda: (0, 0, 0))],
        out_specs=pl.BlockSpec((HEADS,) + buckets.shape, lambda: (0, 0, 0, 0)),
        out_shape=jax.ShapeDtypeStruct((HEADS,) + buckets.shape, F32),
        name="moba_bias",
    )(rel_bias, jnp.asarray(buckets))


def _moba_body(rb_ref, q_ref, k_ref, v_ref, bias_ref, y_ref,
               vt_ref, km_ref, sel_ref, m_ref, l_ref, acc_ref, *, far_bucket):
    i = pl.program_id(1)
    BS = MOBA_BLOCK
    nb = k_ref.shape[1] // BS
    n_sel = min(MOBA_TOPK, nb - 1)
    scale = HEAD_DIM ** -0.5

    @pl.when(i == 0)
    def _():
        hid = _head_of((8, GW), 1)
        for n in range(nb):
            kb = k_ref[0, n * BS:(n + 1) * BS, :].astype(F32)
            vt_ref[:, n * BS:(n + 1) * BS] = jnp.transpose(v_ref[0, n * BS:(n + 1) * BS, :].astype(F32)).astype(BF16)
            km = jnp.mean(kb, axis=0, keepdims=True)
            for h in range(HEADS):
                km_ref[h * nb + n:h * nb + n + 1, :] = jnp.where(hid[0:1] == h, km, 0.0).astype(BF16)

    q = q_ref[0]
    hid_q = _head_of((BS, GW), 1)
    qs = (q.astype(F32) * scale).astype(BF16)
    qh = [jnp.where(hid_q == h, qs, jnp.zeros_like(qs)) for h in range(HEADS)]

    gate = _nt(km_ref[...], q)
    blk = lax.broadcasted_iota(jnp.int32, (nb, BS), 0)
    for h in range(HEADS):
        g = jnp.where(blk < i, gate[h * nb:(h + 1) * nb], NEG_BIG)
        rank = jnp.zeros((nb, BS), jnp.int32)
        for mrow in range(nb):
            gm = g[mrow:mrow + 1]
            ahead = (gm > g) | ((gm == g) & (mrow < blk))
            rank = rank + jnp.where(ahead, 1, 0)
        chosen = (rank < n_sel) & (blk < i)
        far = jnp.where(blk < i - 1, rb_ref[far_bucket, h], 0.0)
        sel_ref[h * nb:(h + 1) * nb, :] = jnp.where(chosen, far, NEG_BIG)

    def attend(kblk, r0, h, extra, first):
        s = _nt(kblk, qh[h]) + extra
        mx = jnp.max(s, axis=0, keepdims=True)
        if first:
            m_new = mx
            p = jnp.exp(s - m_new)
            l_ref[h:h + 1, :] = jnp.sum(p, axis=0, keepdims=True)
            acc_ref[h * HEAD_DIM:(h + 1) * HEAD_DIM, :] = _dot(vt_ref[h * HEAD_DIM:(h + 1) * HEAD_DIM, pl.ds(r0, BS)], p.astype(BF16))
        else:
            m_old = m_ref[h:h + 1, :]
            m_new = jnp.maximum(m_old, mx)
            alpha = jnp.exp(m_old - m_new)
            p = jnp.exp(s - m_new)
            l_ref[h:h + 1, :] = alpha * l_ref[h:h + 1, :] + jnp.sum(p, axis=0, keepdims=True)
            acc_ref[h * HEAD_DIM:(h + 1) * HEAD_DIM, :] = (
                alpha * acc_ref[h * HEAD_DIM:(h + 1) * HEAD_DIM, :]
                + _dot(vt_ref[h * HEAD_DIM:(h + 1) * HEAD_DIM, pl.ds(r0, BS)], p.astype(BF16)))
        m_ref[h:h + 1, :] = m_new

    r_own = pl.multiple_of(i * BS, BS)
    k_own = k_ref[0, pl.ds(r_own, BS), :]
    for h in range(HEADS):
        attend(k_own, r_own, h, bias_ref[h, 0], True)

    @pl.when(i >= 1)
    def _():
        r_adj = pl.multiple_of((i - 1) * BS, BS)
        k_adj = k_ref[0, pl.ds(r_adj, BS), :]
        for h in range(HEADS):
            attend(k_adj, r_adj, h, bias_ref[h, 1] + sel_ref[pl.ds(h * nb + i - 1, 1), :], False)

    def far_block(n, carry):
        r_n = pl.multiple_of(n * BS, BS)
        k_n = k_ref[0, pl.ds(r_n, BS), :]
        for h in range(HEADS):
            attend(k_n, r_n, h, sel_ref[pl.ds(h * nb + n, 1), :], False)
        return carry

    lax.fori_loop(0, jnp.maximum(i - 1, 0), far_block, 0)

    linv = 1.0 / l_ref[...]
    rows = lax.broadcasted_iota(jnp.int32, (GW, BS), 0) // HEAD_DIM
    scale_rows = jnp.where(rows == 0, linv[0:1], jnp.where(rows == 1, linv[1:2], jnp.where(rows == 2, linv[2:3], linv[3:4])))
    y_ref[0] = jnp.transpose(acc_ref[...] * scale_rows).astype(y_ref.dtype)


def _moba(main3, rel_bias, bias_tiles, far_bucket):
    b, s, _ = main3.shape
    nb = s // MOBA_BLOCK
    return pl.pallas_call(
        functools.partial(_moba_body, far_bucket=far_bucket),
        grid=(b, nb),
        in_specs=[pl.BlockSpec(memory_space=pltpu.SMEM),
                  pl.BlockSpec((1, MOBA_BLOCK, GW), lambda bi, i: (bi, i, CB_AQ)),
                  pl.BlockSpec((1, s, GW), lambda bi, i: (bi, 0, CB_AK)),
                  pl.BlockSpec((1, s, GW), lambda bi, i: (bi, 0, CB_AV)),
                  pl.BlockSpec(bias_tiles.shape, lambda bi, i: (0, 0, 0, 0))],
        out_specs=pl.BlockSpec((1, MOBA_BLOCK, GW), lambda bi, i: (bi, i, 0)),
        out_shape=jax.ShapeDtypeStruct((b, s, GW), BF16),
        scratch_shapes=[pltpu.VMEM((GW, s), BF16),
                        pltpu.VMEM((HEADS * nb, GW), BF16),
                        pltpu.VMEM((HEADS * nb, MOBA_BLOCK), F32),
                        pltpu.VMEM((8, MOBA_BLOCK), F32),
                        pltpu.VMEM((8, MOBA_BLOCK), F32),
                        pltpu.VMEM((GW, MOBA_BLOCK), F32)],
        compiler_params=_params("parallel", "arbitrary"),
        name="moba",
    )(rel_bias, main3, main3, main3, bias_tiles)


def _outproj_body(ym_ref, yc_ref, ya_ref, yh_ref, w_ref, g_ref, x_ref, o_ref):
    h = (_dot(ym_ref[...], w_ref[0 * GW:1 * GW, :]) + _dot(yc_ref[...], w_ref[1 * GW:2 * GW, :])
         + _dot(ya_ref[...], w_ref[2 * GW:3 * GW, :]) + _dot(yh_ref[...], w_ref[3 * GW:4 * GW, :]))
    o_ref[...] = x_ref[...] + _rms(h, g_ref[...])


def _outproj(ys, w, gain, x2, tm):
    m, d = x2.shape
    row = lambda width: pl.BlockSpec((tm, width), lambda i: (i, 0))
    const = lambda i: (0, 0)
    return pl.pallas_call(
        _outproj_body,
        grid=(m // tm,),
        in_specs=[row(GW), row(GW), row(GW), row(GW),
                  pl.BlockSpec(w.shape, const), pl.BlockSpec((1, d), const), row(d)],
        out_specs=row(d),
        out_shape=jax.ShapeDtypeStruct((m, d), F32),
        compiler_params=_params("parallel"),
        name="outproj",
    )(*ys, w, gain, x2)


def _memkv_body(mem_ref, g_ref, wk_ref, wv_ref, k_ref, v_ref):
    mn = _rms(mem_ref[0], g_ref[...]).astype(BF16)
    k_ref[0] = _dot(mn, wk_ref[...]).astype(BF16)
    v_ref[0] = _dot(mn, wv_ref[...]).astype(BF16)


def _memkv(mem, gain, wk, wv):
    b, nm, d = mem.shape
    cw = wk.shape[1]
    const = lambda i: (0, 0)
    return pl.pallas_call(
        _memkv_body,
        grid=(b,),
        in_specs=[pl.BlockSpec((1, nm, d), lambda i: (i, 0, 0)), pl.BlockSpec((1, d), const),
                  pl.BlockSpec(wk.shape, const), pl.BlockSpec(wv.shape, const)],
        out_specs=[pl.BlockSpec((1, nm, cw), lambda i: (i, 0, 0))] * 2,
        out_shape=[jax.ShapeDtypeStruct((b, nm, cw), BF16)] * 2,
        compiler_params=_params("parallel"),
        name="memkv",
    )(mem, gain, wk, wv)


def _cross_body(x_ref, gpre_ref, wq_ref, k_ref, v_ref, wo_ref, gpost_ref, o_ref):
    x = x_ref[0]
    xn = _rms(x, gpre_ref[...]).astype(BF16)
    q = _dot(xn, wq_ref[...]).astype(BF16)
    k = k_ref[0]
    v = v_ref[0]
    outs = []
    for h in range(CROSS_HEADS):
        sl = slice(h * CROSS_HEAD_DIM, (h + 1) * CROSS_HEAD_DIM)
        s = _nt(q[:, sl], k[:, sl]) * (CROSS_HEAD_DIM ** -0.5)
        p = jnp.exp(s - jnp.max(s, axis=-1, keepdims=True))
        p = p / jnp.sum(p, axis=-1, keepdims=True)
        outs.append(_dot(p.astype(BF16), v[:, sl]).astype(BF16))
    o = jnp.concatenate(outs, axis=-1)
    o_ref[0] = x + _rms(_dot(o, wo_ref[...]), gpost_ref[...])


def _cross(x3, gpre, wq, km, vm, wo, gpost, tq):
    b, s, d = x3.shape
    nm, cw = km.shape[1:]
    const = lambda bi, i: (0, 0)
    return pl.pallas_call(
        _cross_body,
        grid=(b, s // tq),
        in_specs=[pl.BlockSpec((1, tq, d), lambda bi, i: (bi, i, 0)), pl.BlockSpec((1, d), const),
                  pl.BlockSpec(wq.shape, const),
                  pl.BlockSpec((1, nm, cw), lambda bi, i: (bi, 0, 0)),
                  pl.BlockSpec((1, nm, cw), lambda bi, i: (bi, 0, 0)),
                  pl.BlockSpec(wo.shape, const), pl.BlockSpec((1, d), const)],
        out_specs=pl.BlockSpec((1, tq, d), lambda bi, i: (bi, i, 0)),
        out_shape=jax.ShapeDtypeStruct((b, s, d), F32),
        compiler_params=_params("parallel", "parallel"),
        name="cross",
    )(x3, gpre, wq, km, vm, wo, gpost)


def _ffn_body(x_ref, xh_ref, gpre_ref, wg_ref, wu_ref, cw_ref, cb_ref, wd_ref, gpost_ref, o_ref,
              xn_ref, acc_ref, *, seq_len):
    i = pl.program_id(0)
    j = pl.program_id(1)
    tm = x_ref.shape[0]

    @pl.when(j == 0)
    def _():
        g = gpre_ref[...]
        xn_ref[8:, :] = _rms(x_ref[...], g).astype(BF16)
        halo = _rms(xh_ref[...], g)
        starts_seq = (i * tm) % seq_len == 0
        xn_ref[0:8, :] = jnp.where(starts_seq, 0.0, halo).astype(BF16)
        acc_ref[...] = jnp.zeros_like(acc_ref)

    xe = xn_ref[...]
    ge = _dot(xe, wg_ref[...])
    up = _dot(xe[8:], wu_ref[...])
    cw = cw_ref[...]
    gate = cw[0:1] * ge[6:6 + tm] + cw[1:2] * ge[7:7 + tm] + cw[2:3] * ge[8:8 + tm] + cb_ref[...]
    act = (jax.nn.silu(gate) * up).astype(BF16)
    acc_ref[...] += _dot(act, wd_ref[...])

    @pl.when(j == pl.num_programs(1) - 1)
    def _():
        o_ref[...] = x_ref[...] + _rms(acc_ref[...], gpost_ref[...])


def _ffn(x2, gpre, wg, wu, cw, cb, wd, gpost, seq_len, tm, tf):
    m, d = x2.shape
    f = wg.shape[1]
    const = lambda i, j: (0, 0)
    return pl.pallas_call(
        functools.partial(_ffn_body, seq_len=seq_len),
        grid=(m // tm, f // tf),
        in_specs=[pl.BlockSpec((tm, d), lambda i, j: (i, 0)),
                  pl.BlockSpec((8, d), lambda i, j: (jnp.maximum(i * (tm // 8) - 1, 0), 0)),
                  pl.BlockSpec((1, d), const),
                  pl.BlockSpec((d, tf), lambda i, j: (0, j)),
                  pl.BlockSpec((d, tf), lambda i, j: (0, j)),
                  pl.BlockSpec((3, tf), lambda i, j: (0, j)),
                  pl.BlockSpec((1, tf), lambda i, j: (0, j)),
                  pl.BlockSpec((tf, d), lambda i, j: (j, 0)),
                  pl.BlockSpec((1, d), const)],
        out_specs=pl.BlockSpec((tm, d), lambda i, j: (i, 0)),
        out_shape=jax.ShapeDtypeStruct((m, d), F32),
        scratch_shapes=[pltpu.VMEM((tm + 8, d), BF16), pltpu.VMEM((tm, d), F32)],
        compiler_params=_params("parallel", "arbitrary"),
        name="ffn",
    )(x2, x2, gpre, wg, wu, cw, cb, wd, gpost)


def _row_tile(m, pref):
    t = pref
    while m % t:
        t //= 2
    return t


def kernel(x, mem, w_in, b_in, mlstm_norm, sconv_w, rel_bias, hgrn_lb_logits, hgrn_norm, w_mix_out, norm_mix_pre, norm_mix_post, mem_norm, w_cq, w_ck, w_cv, w_co, norm_cross_pre, norm_cross_post, w_ffn_in, ffn_conv_w, ffn_conv_b, w_ffn_out, norm_ffn_pre, norm_ffn_post):
    b, s, d = x.shape
    depth = w_in.shape[0]
    d_ff = w_ffn_out.shape[1]
    assert s % MOBA_BLOCK == 0 and s % MLSTM_CHUNK == 0 and s % HGRN_CHUNK == 0
    m = b * s
    tm = _row_tile(m, 512)

    gate0 = 4 * GW
    hq0 = gate0 + 2 * HEADS + 6 * GW
    main_cols = lambda a: jnp.concatenate([a[..., :gate0], a[..., gate0 + 2 * HEADS:hq0 + GW], a[..., hq0 + 2 * GW:]], axis=-1)
    aux_cols = lambda a: jnp.concatenate(
        [a[..., hq0 + GW:hq0 + 2 * GW], a[..., gate0:gate0 + 2 * HEADS],
         jnp.zeros(a.shape[:-1] + (N_AUX - GW - 2 * HEADS,), a.dtype)], axis=-1)
    wm_all = main_cols(w_in).astype(BF16)
    wa_all = aux_cols(w_in).astype(BF16)
    bm_all = main_cols(b_in)[:, None, :]
    ba_all = aux_cols(b_in)[:, None, :]

    w_out = w_mix_out.astype(BF16)
    wq, wk, wv, wo = (w.astype(BF16) for w in (w_cq, w_ck, w_cv, w_co))
    w_gate = w_ffn_in[..., :d_ff].astype(BF16)
    w_up = w_ffn_in[..., d_ff:].astype(BF16)
    w_down = w_ffn_out.astype(BF16)
    row = lambda a, l: a[l][None, :]

    buckets, far_bucket = _bucket_tables()
    bias_tiles = _bias_tiles(rel_bias, buckets)
    tf = 256 if d_ff % 256 == 0 else 128

    x2 = x.reshape(m, d)
    for l in range(depth):
        main, aux = _inproj(x2, row(norm_mix_pre, l), wm_all[l], bm_all[l], wa_all[l], ba_all[l], tm)
        main3 = main.reshape(b, s, N_MAIN)
        aux3 = aux.reshape(b, s, N_AUX)
        y_m = _mlstm(main3, aux3, row(mlstm_norm, l))
        y_c = _sconv(main3, sconv_w[l])
        y_a = _moba(main3, rel_bias, bias_tiles, far_bucket)
        y_h = _hgrn(main3, aux3, hgrn_lb_logits, row(hgrn_norm, l), l)
        ys = [y.reshape(m, GW) for y in (y_m, y_c, y_a, y_h)]
        x2 = _outproj(ys, w_out[l], row(norm_mix_post, l), x2, tm)

        km, vm = _memkv(mem, row(mem_norm, l), wk[l], wv[l])
        x2 = _cross(x2.reshape(b, s, d), row(norm_cross_pre, l), wq[l], km, vm, wo[l],
                    row(norm_cross_post, l), _row_tile(s, 512)).reshape(m, d)

        x2 = _ffn(x2, row(norm_ffn_pre, l), w_gate[l], w_up[l], ffn_conv_w[l], row(ffn_conv_b, l),
                  w_down[l], row(norm_ffn_post, l), s, tm, tf)
    return x2.reshape(b, s, d)
```

```python
import functools
import math

import numpy as np
import jax
import jax.numpy as jnp
from jax import lax
from jax.experimental import pallas as pl
from jax.experimental.pallas import tpu as pltpu

F32 = jnp.float32
BF16 = jnp.bfloat16

HEADS = 4
HEAD_DIM = 64
GW = HEADS * HEAD_DIM
MOBA_BLOCK = 256
MOBA_TOPK = 3
REL_BUCKETS = 32
REL_MAX_DIST = 128
CROSS_HEADS = 4
CROSS_HEAD_DIM = 128
RMS_EPS = 1e-6
NEG_BIG = -1e30

MLSTM_CHUNK = 256
MLSTM_BATCH = 1
HGRN_CHUNK = 64
FFN_CHUNK = 256
VMEM_LIMIT = 56 * 1024 * 1024

(CB_MQ, CB_MK, CB_MV, CB_MO, CB_CB, CB_CC, CB_CH, CB_AQ, CB_AK, CB_AV,
 CB_HQ, CB_HI, CB_HG) = range(13)
N_MAIN = 13 * GW
N_AUX = 512


def _rms(x, g):
    return x * lax.rsqrt(jnp.mean(x * x, axis=-1, keepdims=True) + RMS_EPS) * g


def _nt(a, b):
    return lax.dot_general(a, b, (((1,), (1,)), ((), ())), preferred_element_type=F32)


def _tn(a, b):
    return lax.dot_general(a, b, (((0,), (0,)), ((), ())), preferred_element_type=F32)


def _dot(a, b):
    return jnp.dot(a, b, preferred_element_type=F32)


def _head_of(shape, axis):
    return lax.shift_right_logical(lax.broadcasted_iota(jnp.int32, shape, axis), 6)


def _expand_heads(cols, width=GW):
    rows = cols[0].shape[0]
    hid = _head_of((rows, width), 1)
    return jnp.where(hid == 0, cols[0], jnp.where(hid == 1, cols[1], jnp.where(hid == 2, cols[2], cols[3])))


def _params(*sem):
    return pltpu.CompilerParams(dimension_semantics=sem, vmem_limit_bytes=VMEM_LIMIT)


def _inproj_body(x_ref, g_ref, wm_ref, bm_ref, wa_ref, ba_ref, om_ref, oa_ref):
    xn = _rms(x_ref[...], g_ref[...]).astype(BF16)
    for c in range(0, om_ref.shape[-1], GW):
        acc = _dot(xn, wm_ref[:, c:c + GW]) + bm_ref[:, c:c + GW]
        om_ref[:, c:c + GW] = acc.astype(om_ref.dtype)
    oa_ref[...] = _dot(xn, wa_ref[...]) + ba_ref[...]


def _inproj(x2, gain, wm, bm, wa, ba, tm):
    m, d = x2.shape
    const = lambda i: (0, 0)
    return pl.pallas_call(
        _inproj_body,
        grid=(m // tm,),
        in_specs=[
            pl.BlockSpec((tm, d), lambda i: (i, 0)),
            pl.BlockSpec((1, d), const),
            pl.BlockSpec((d, N_MAIN), const),
            pl.BlockSpec((1, N_MAIN), const),
            pl.BlockSpec((d, N_AUX), const),
            pl.BlockSpec((1, N_AUX), const),
        ],
        out_specs=[
            pl.BlockSpec((tm, N_MAIN), lambda i: (i, 0)),
            pl.BlockSpec((tm, N_AUX), lambda i: (i, 0)),
        ],
        out_shape=[
            jax.ShapeDtypeStruct((m, N_MAIN), BF16),
            jax.ShapeDtypeStruct((m, N_AUX), F32),
        ],
        compiler_params=_params("parallel"),
        name="inproj",
    )(x2, gain, wm, bm, wa, ba)


def _sconv_body(b_ref, c_ref, h_ref, w_ref, o_ref):
    u = c_ref[0].astype(F32) * h_ref[0].astype(F32)
    row = lax.broadcasted_iota(jnp.int32, u.shape, 0)
    u1 = jnp.where(row >= 1, pltpu.roll(u, 1, 0), 0.0)
    u2 = jnp.where(row >= 2, pltpu.roll(u, 2, 0), 0.0)
    w = w_ref[...]
    y = w[0:1] * u2 + w[1:2] * u1 + w[2:3] * u
    o_ref[0] = (b_ref[0].astype(F32) * y).astype(o_ref.dtype)


def _sconv(main3, w):
    b, s, _ = main3.shape
    col = lambda cb: pl.BlockSpec((1, s, GW), lambda i: (i, 0, cb))
    return pl.pallas_call(
        _sconv_body,
        grid=(b,),
        in_specs=[col(CB_CB), col(CB_CC), col(CB_CH), pl.BlockSpec(w.shape, lambda i: (0, 0))],
        out_specs=pl.BlockSpec((1, s, GW), lambda i: (i, 0, 0)),
        out_shape=jax.ShapeDtypeStruct((b, s, GW), BF16),
        compiler_params=_params("parallel"),
        name="sconv",
    )(main3, main3, main3, w)


def _scan_lanes(x, op):
    n = x.shape[1]
    lane = lax.broadcasted_iota(jnp.int32, x.shape, 1)
    s = 1
    while s < n:
        x = jnp.where(lane >= s, op(x, pltpu.roll(x, s, 1)), x)
        s *= 2
    return x


def _head_rms_gate(hval, gain, gate, bones):
    sq = hval * hval
    hi = sq.astype(BF16)
    lo = (sq - hi.astype(F32)).astype(BF16)
    ms = (_dot(hi, bones) + _dot(lo, bones)) * (1.0 / HEAD_DIM)
    return hval * lax.rsqrt(ms + RMS_EPS) * gain * gate


def _block_ones():
    r = _head_of((GW, GW), 0)
    c = _head_of((GW, GW), 1)
    return r == c


def _mlstm_body(q_ref, k_ref, v_ref, o_ref, g_ref, gain_ref, y_ref, ct_ref, n_ref, m_ref):
    s_len = q_ref.shape[1]
    L = MLSTM_CHUNK
    bd = _block_ones()
    bones = jnp.where(bd, 1.0, 0.0).astype(BF16)
    hid_l = _head_of((L, GW), 1)
    hmask = [hid_l == h for h in range(HEADS)]
    tril = lax.broadcasted_iota(jnp.int32, (L, L), 0) >= lax.broadcasted_iota(jnp.int32, (L, L), 1)
    row8 = lax.broadcasted_iota(jnp.int32, (8, L), 0)
    nrow = lax.broadcasted_iota(jnp.int32, (8, GW), 0)
    nkeep = nrow == _head_of((8, GW), 1)
    gain = gain_ref[...]

    ct_ref[...] = jnp.zeros_like(ct_ref)
    n_ref[...] = jnp.zeros_like(n_ref)
    m_ref[...] = jnp.zeros_like(m_ref)

    def one_batch(bb, r0):
        q = q_ref[bb, pl.ds(r0, L), :]
        k = (k_ref[bb, pl.ds(r0, L), :].astype(F32) * (HEAD_DIM ** -0.5)).astype(BF16)
        v = v_ref[bb, pl.ds(r0, L), :]
        gt = jnp.transpose(g_ref[bb, pl.ds(r0, L), :])[0:8]
        li = gt
        lf = jax.nn.log_sigmoid(pltpu.roll(gt, 4, 0))
        g = _scan_lanes(lf, jnp.add)
        cc = li - g
        mprev = m_ref[bb][:, 0:1]
        pp = jnp.maximum(mprev, _scan_lanes(cc, jnp.maximum))
        ein = jnp.exp(mprev - pp)
        p_last = pp[:, L - 1:L]
        ws = jnp.exp(cc - p_last)
        decay = jnp.exp(mprev - p_last)
        m_new = g[:, L - 1:L] + p_last

        b0 = jnp.where(row8 < 4, pp, pltpu.roll(ein, 4, 0))
        b1 = jnp.where(row8 < 4, g + pp, pltpu.roll(ws, 4, 0))
        cols = jnp.transpose(jnp.concatenate([b0, b1, jnp.zeros((112, L), F32)], axis=0))

        ct = ct_ref[bb]
        qc = _nt(q, ct.astype(BF16))
        qn = _nt(q, n_ref[bb].astype(BF16))
        num = jnp.zeros((L, GW), F32)
        inv_cols = []
        for h in range(HEADS):
            qh = jnp.where(hmask[h], q, jnp.zeros_like(q))
            sc = _nt(qh, k)
            dlog = jnp.where(tril, cc[h:h + 1, :] - cols[:, h:h + 1], NEG_BIG)
            w = jnp.exp(dlog) * sc
            den = jnp.sum(w, axis=-1, keepdims=True) + cols[:, 4 + h:5 + h] * qn[:, h:h + 1]
            inv_cols.append(1.0 / jnp.maximum(jnp.abs(den), jnp.exp(-cols[:, 8 + h:9 + h])))
            num = num + jnp.where(hmask[h], _dot(w.astype(BF16), v), 0.0)
        num = num + _expand_heads([cols[:, 4 + h:5 + h] for h in range(HEADS)]) * qc
        hval = num * _expand_heads(inv_cols)
        gate = jax.nn.sigmoid(o_ref[bb, pl.ds(r0, L), :].astype(F32))
        y_ref[bb, pl.ds(r0, L), :] = _head_rms_gate(hval, gain, gate, bones).astype(y_ref.dtype)

        vw = (v.astype(F32) * _expand_heads([cols[:, 12 + h:13 + h] for h in range(HEADS)])).astype(BF16)
        hid1 = _head_of((1, GW), 1)
        dec_l = jnp.where(hid1 == 0, decay[0:1], jnp.where(hid1 == 1, decay[1:2], jnp.where(hid1 == 2, decay[2:3], decay[3:4])))
        ct_ref[bb] = ct * dec_l + jnp.where(bd, _tn(vw, k), 0.0)
        n_ref[bb] = n_ref[bb] * decay + jnp.where(nkeep, _dot(ws.astype(BF16), k), 0.0)
        m_ref[bb] = jnp.broadcast_to(m_new, m_ref.shape[1:])

    def step(c, carry):
        r0 = pl.multiple_of(c * L, L)
        for bb in range(q_ref.shape[0]):
            one_batch(bb, r0)
        return carry

    lax.fori_loop(0, s_len // L, step, 0)


def _mlstm(main3, aux3, gain):
    b, s, _ = main3.shape
    nb = MLSTM_BATCH if b % MLSTM_BATCH == 0 else 1
    col = lambda cb: pl.BlockSpec((nb, s, GW), lambda i: (i, 0, cb))
    return pl.pallas_call(
        _mlstm_body,
        grid=(b // nb,),
        in_specs=[col(CB_MQ), col(CB_MK), col(CB_MV), col(CB_MO),
                  pl.BlockSpec((nb, s, 128), lambda i: (i, 0, 2)),
                  pl.BlockSpec((1, GW), lambda i: (0, 0))],
        out_specs=pl.BlockSpec((nb, s, GW), lambda i: (i, 0, 0)),
        out_shape=jax.ShapeDtypeStruct((b, s, GW), BF16),
        scratch_shapes=[pltpu.VMEM((nb, GW, GW), F32), pltpu.VMEM((nb, 8, GW), F32), pltpu.VMEM((nb, 8, 128), F32)],
        compiler_params=_params("parallel"),
        name="mlstm",
    )(main3, main3, main3, main3, aux3, gain)


def _seg_prefix(x, seg, pos):
    s = 1
    while s < seg:
        x = x + jnp.where(pos >= s, pltpu.roll(x, s, 0), 0.0)
        s *= 2
    return x


def _seg_suffix(x, seg, pos):
    n = x.shape[0]
    s = 1
    while s < seg:
        x = x + jnp.where(pos + s < seg, pltpu.roll(x, n - s, 0), 0.0)
        s *= 2
    return x


def _hgrn_body(lbl_ref, q_ref, i_ref, g_ref, f_ref, gain_ref, y_ref, st_ref, *, layer):
    s_len = q_ref.shape[1]
    C = HGRN_CHUNK
    bd = _block_ones()
    bones = jnp.where(bd, 1.0, 0.0).astype(BF16)
    hid_c = _head_of((C, GW), 1)
    hmask = [hid_c == h for h in range(HEADS)]
    gain = gain_ref[...]

    lg = lbl_ref[...]
    ex = jnp.exp(lg - jnp.max(lg, axis=0, keepdims=True))
    soft = ex / jnp.sum(ex, axis=0, keepdims=True)
    lb = jnp.zeros((1, GW), F32)
    for j in range(1, layer + 1):
        lb = lb + soft[j:j + 1]

    tq = lax.broadcasted_iota(jnp.int32, (HEADS * C, C), 0) & (C - 1)
    tk = lax.broadcasted_iota(jnp.int32, (HEADS * C, C), 1)
    levels = []
    bsz = C
    while bsz >= 2:
        half = bsz // 2
        same = (tq // bsz) == (tk // bsz)
        levels.append((bsz, same & ((tq % bsz) >= half) & ((tk % bsz) < half)))
        bsz = half
    eye = tq == tk
    rowpos = lax.broadcasted_iota(jnp.int32, (C, GW), 0)

    st_ref[...] = jnp.zeros_like(st_ref)

    def stack_heads(a):
        return jnp.concatenate([jnp.where(hmask[h], a, 0.0) for h in range(HEADS)], axis=0).astype(BF16)

    def step(c, carry):
        r0 = pl.multiple_of(c * C, C)
        z = f_ref[0, pl.ds(r0, C), :]
        f = lb + (1.0 - lb) * jax.nn.sigmoid(z)
        lf = jnp.log(f)
        kk = 1.0 - f
        qq = jax.nn.silu(q_ref[0, pl.ds(r0, C), :].astype(F32))
        vv = i_ref[0, pl.ds(r0, C), :]

        gcum = _seg_prefix(lf, C, rowpos)
        st = st_ref[...]
        out = _nt((qq * jnp.exp(gcum)).astype(BF16), st.astype(BF16))

        att = jnp.where(eye, _nt(stack_heads(qq), kk.astype(BF16)), 0.0)
        for bsz, msk in levels:
            half = bsz // 2
            pos = rowpos & (half - 1)
            qd = qq * jnp.exp(_seg_prefix(lf, half, pos))
            kd = kk * jnp.exp(_seg_suffix(lf, half, pos) - lf)
            att = att + jnp.where(msk, _nt(stack_heads(qd), kd.astype(BF16)), 0.0)
        for h in range(HEADS):
            out = out + jnp.where(hmask[h], _dot(att[h * C:(h + 1) * C].astype(BF16), vv), 0.0)

        gate = jax.nn.silu(g_ref[0, pl.ds(r0, C), :].astype(F32))
        y_ref[0, pl.ds(r0, C), :] = _head_rms_gate(out, gain, gate, bones).astype(y_ref.dtype)

        g_last = gcum[C - 1:C, :]
        kdec = (kk * jnp.exp(g_last - gcum)).astype(BF16)
        st_ref[...] = st * jnp.exp(g_last) + jnp.where(bd, _tn(vv, kdec), 0.0)
        return carry

    lax.fori_loop(0, s_len // C, step, 0)


def _hgrn(main3, aux3, lb_logits, gain, layer):
    b, s, _ = main3.shape
    col = lambda cb: pl.BlockSpec((1, s, GW), lambda i: (i, 0, cb))
    return pl.pallas_call(
        functools.partial(_hgrn_body, layer=layer),
        grid=(b,),
        in_specs=[pl.BlockSpec(lb_logits.shape, lambda i: (0, 0)),
                  col(CB_HQ), col(CB_HI), col(CB_HG),
                  pl.BlockSpec((1, s, GW), lambda i: (i, 0, 0)),
                  pl.BlockSpec((1, GW), lambda i: (0, 0))],
        out_specs=pl.BlockSpec((1, s, GW), lambda i: (i, 0, 0)),
        out_shape=jax.ShapeDtypeStruct((b, s, GW), BF16),
        scratch_shapes=[pltpu.VMEM((GW, GW), F32)],
        compiler_params=_params("parallel"),
        name="hgrn2",
    )(lb_logits, main3, main3, main3, aux3, gain)


def _bucket_tables():
    def bucket(dist):
        n = np.maximum(dist, 0)
        exact = REL_BUCKETS // 2
        nf = np.maximum(n, 1).astype(np.float32)
        large = exact + (np.log(nf / np.float32(exact)) / np.float32(math.log(REL_MAX_DIST / exact))
                         * np.float32(REL_BUCKETS - exact)).astype(np.int32)
        large = np.minimum(large, REL_BUCKETS - 1)
        return np.where(n < exact, n, large).astype(np.int32)
    tk = np.arange(MOBA_BLOCK)[:, None]
    tq = np.arange(MOBA_BLOCK)[None, :]
    own = np.where(tq - tk >= 0, bucket(tq - tk), REL_BUCKETS)
    adj = bucket(MOBA_BLOCK + tq - tk)
    far = int(bucket(np.array([2 * MOBA_BLOCK]))[0])
    return np.stack([own, adj]).astype(np.int32), far


def _bias_body(rb_ref, bk_ref, o_ref):
    bk = bk_ref[...]
    for h in range(HEADS):
        acc = jnp.full(bk.shape, NEG_BIG, F32)
        for j in range(REL_BUCKETS):
            acc = jnp.where(bk == j, rb_ref[j, h], acc)
        o_ref[h] = acc


def _bias_tiles(rel_bias, buckets):
    return pl.pallas_call(
        _bias_body,
        in_specs=[pl.BlockSpec(memory_space=pltpu.SMEM), pl.BlockSpec(buckets.shape, lambda: (0, 0, 0))],
        out_specs=pl.BlockSpec((HEADS,) + buckets.shape, lambda: (0, 0, 0, 0)),
        out_shape=jax.ShapeDtypeStruct((HEADS,) + buckets.shape, F32),
        name="moba_bias",
    )(rel_bias, jnp.asarray(buckets))


def _moba_body(rb_ref, q_ref, k_ref, v_ref, bias_ref, y_ref,
               vt_ref, km_ref, sel_ref, m_ref, l_ref, acc_ref, *, far_bucket):
    i = pl.program_id(1)
    BS = MOBA_BLOCK
    nb = k_ref.shape[1] // BS
    n_sel = min(MOBA_TOPK, nb - 1)
    scale = HEAD_DIM ** -0.5

    @pl.when(i == 0)
    def _():
        hid = _head_of((8, GW), 1)
        for n in range(nb):
            kb = k_ref[0, n * BS:(n + 1) * BS, :].astype(F32)
            vt_ref[:, n * BS:(n + 1) * BS] = jnp.transpose(v_ref[0, n * BS:(n + 1) * BS, :].astype(F32)).astype(BF16)
            km = jnp.mean(kb, axis=0, keepdims=True)
            for h in range(HEADS):
                km_ref[h * nb + n:h * nb + n + 1, :] = jnp.where(hid[0:1] == h, km, 0.0).astype(BF16)

    q = q_ref[0]
    hid_q = _head_of((BS, GW), 1)
    qs = (q.astype(F32) * scale).astype(BF16)
    qh = [jnp.where(hid_q == h, qs, jnp.zeros_like(qs)) for h in range(HEADS)]

    gate = _nt(km_ref[...], q)
    blk = lax.broadcasted_iota(jnp.int32, (nb, BS), 0)
    for h in range(HEADS):
        g = jnp.where(blk < i, gate[h * nb:(h + 1) * nb], NEG_BIG)
        rank = jnp.zeros((nb, BS), jnp.int32)
        for mrow in range(nb):
            gm = g[mrow:mrow + 1]
            ahead = (gm > g) | ((gm == g) & (mrow < blk))
            rank = rank + jnp.where(ahead, 1, 0)
        chosen = (rank < n_sel) & (blk < i)
        far = jnp.where(blk < i - 1, rb_ref[far_bucket, h], 0.0)
        sel_ref[h * nb:(h + 1) * nb, :] = jnp.where(chosen, far, NEG_BIG)

    def attend(kblk, r0, h, extra, first):
        s = _nt(kblk, qh[h]) + extra
        mx = jnp.max(s, axis=0, keepdims=True)
        if first:
            m_new = mx
            p = jnp.exp(s - m_new)
            l_ref[h:h + 1, :] = jnp.sum(p, axis=0, keepdims=True)
            acc_ref[h * HEAD_DIM:(h + 1) * HEAD_DIM, :] = _dot(vt_ref[h * HEAD_DIM:(h + 1) * HEAD_DIM, pl.ds(r0, BS)], p.astype(BF16))
        else:
            m_old = m_ref[h:h + 1, :]
            m_new = jnp.maximum(m_old, mx)
            alpha = jnp.exp(m_old - m_new)
            p = jnp.exp(s - m_new)
            l_ref[h:h + 1, :] = alpha * l_ref[h:h + 1, :] + jnp.sum(p, axis=0, keepdims=True)
            acc_ref[h * HEAD_DIM:(h + 1) * HEAD_DIM, :] = (
                alpha * acc_ref[h * HEAD_DIM:(h + 1) * HEAD_DIM, :]
                + _dot(vt_ref[h * HEAD_DIM:(h + 1) * HEAD_DIM, pl.ds(r0, BS)], p.astype(BF16)))
        m_ref[h:h + 1, :] = m_new

    r_own = pl.multiple_of(i * BS, BS)
    k_own = k_ref[0, pl.ds(r_own, BS), :]
    for h in range(HEADS):
        attend(k_own, r_own, h, bias_ref[h, 0], True)

    @pl.when(i >= 1)
    def _():
        r_adj = pl.multiple_of((i - 1) * BS, BS)
        k_adj = k_ref[0, pl.ds(r_adj, BS), :]
        for h in range(HEADS):
            attend(k_adj, r_adj, h, bias_ref[h, 1] + sel_ref[pl.ds(h * nb + i - 1, 1), :], False)

    def far_block(n, carry):
        r_n = pl.multiple_of(n * BS, BS)
        k_n = k_ref[0, pl.ds(r_n, BS), :]
        for h in range(HEADS):
            attend(k_n, r_n, h, sel_ref[pl.ds(h * nb + n, 1), :], False)
        return carry

    lax.fori_loop(0, jnp.maximum(i - 1, 0), far_block, 0)

    linv = 1.0 / l_ref[...]
    rows = lax.broadcasted_iota(jnp.int32, (GW, BS), 0) // HEAD_DIM
    scale_rows = jnp.where(rows == 0, linv[0:1], jnp.where(rows == 1, linv[1:2], jnp.where(rows == 2, linv[2:3], linv[3:4])))
    y_ref[0] = jnp.transpose(acc_ref[...] * scale_rows).astype(y_ref.dtype)


def _moba(main3, rel_bias, bias_tiles, far_bucket):
    b, s, _ = main3.shape
    nb = s // MOBA_BLOCK
    return pl.pallas_call(
        functools.partial(_moba_body, far_bucket=far_bucket),
        grid=(b, nb),
        in_specs=[pl.BlockSpec(memory_space=pltpu.SMEM),
                  pl.BlockSpec((1, MOBA_BLOCK, GW), lambda bi, i: (bi, i, CB_AQ)),
                  pl.BlockSpec((1, s, GW), lambda bi, i: (bi, 0, CB_AK)),
                  pl.BlockSpec((1, s, GW), lambda bi, i: (bi, 0, CB_AV)),
                  pl.BlockSpec(bias_tiles.shape, lambda bi, i: (0, 0, 0, 0))],
        out_specs=pl.BlockSpec((1, MOBA_BLOCK, GW), lambda bi, i: (bi, i, 0)),
        out_shape=jax.ShapeDtypeStruct((b, s, GW), BF16),
        scratch_shapes=[pltpu.VMEM((GW, s), BF16),
                        pltpu.VMEM((HEADS * nb, GW), BF16),
                        pltpu.VMEM((HEADS * nb, MOBA_BLOCK), F32),
                        pltpu.VMEM((8, MOBA_BLOCK), F32),
                        pltpu.VMEM((8, MOBA_BLOCK), F32),
                        pltpu.VMEM((GW, MOBA_BLOCK), F32)],
        compiler_params=_params("parallel", "arbitrary"),
        name="moba",
    )(rel_bias, main3, main3, main3, bias_tiles)


def _outproj_body(ym_ref, yc_ref, ya_ref, yh_ref, w_ref, g_ref, x_ref, o_ref):
    h = (_dot(ym_ref[...], w_ref[0 * GW:1 * GW, :]) + _dot(yc_ref[...], w_ref[1 * GW:2 * GW, :])
         + _dot(ya_ref[...], w_ref[2 * GW:3 * GW, :]) + _dot(yh_ref[...], w_ref[3 * GW:4 * GW, :]))
    o_ref[...] = x_ref[...] + _rms(h, g_ref[...])


def _outproj(ys, w, gain, x2, tm):
    m, d = x2.shape
    row = lambda width: pl.BlockSpec((tm, width), lambda i: (i, 0))
    const = lambda i: (0, 0)
    return pl.pallas_call(
        _outproj_body,
        grid=(m // tm,),
        in_specs=[row(GW), row(GW), row(GW), row(GW),
                  pl.BlockSpec(w.shape, const), pl.BlockSpec((1, d), const), row(d)],
        out_specs=row(d),
        out_shape=jax.ShapeDtypeStruct((m, d), F32),
        compiler_params=_params("parallel"),
        name="outproj",
    )(*ys, w, gain, x2)


def _memkv_body(mem_ref, g_ref, wk_ref, wv_ref, k_ref, v_ref):
    mn = _rms(mem_ref[0], g_ref[...]).astype(BF16)
    k_ref[0] = _dot(mn, wk_ref[...]).astype(BF16)
    v_ref[0] = _dot(mn, wv_ref[...]).astype(BF16)


def _memkv(mem, gain, wk, wv):
    b, nm, d = mem.shape
    cw = wk.shape[1]
    const = lambda i: (0, 0)
    return pl.pallas_call(
        _memkv_body,
        grid=(b,),
        in_specs=[pl.BlockSpec((1, nm, d), lambda i: (i, 0, 0)), pl.BlockSpec((1, d), const),
                  pl.BlockSpec(wk.shape, const), pl.BlockSpec(wv.shape, const)],
        out_specs=[pl.BlockSpec((1, nm, cw), lambda i: (i, 0, 0))] * 2,
        out_shape=[jax.ShapeDtypeStruct((b, nm, cw), BF16)] * 2,
        compiler_params=_params("parallel"),
        name="memkv",
    )(mem, gain, wk, wv)


def _cross_body(x_ref, gpre_ref, wq_ref, k_ref, v_ref, wo_ref, gpost_ref, o_ref):
    x = x_ref[0]
    xn = _rms(x, gpre_ref[...]).astype(BF16)
    q = _dot(xn, wq_ref[...]).astype(BF16)
    k = k_ref[0]
    v = v_ref[0]
    outs = []
    for h in range(CROSS_HEADS):
        sl = slice(h * CROSS_HEAD_DIM, (h + 1) * CROSS_HEAD_DIM)
        s = _nt(q[:, sl], k[:, sl]) * (CROSS_HEAD_DIM ** -0.5)
        p = jnp.exp(s - jnp.max(s, axis=-1, keepdims=True))
        p = p / jnp.sum(p, axis=-1, keepdims=True)
        outs.append(_dot(p.astype(BF16), v[:, sl]).astype(BF16))
    o = jnp.concatenate(outs, axis=-1)
    o_ref[0] = x + _rms(_dot(o, wo_ref[...]), gpost_ref[...])


def _cross(x3, gpre, wq, km, vm, wo, gpost, tq):
    b, s, d = x3.shape
    nm, cw = km.shape[1:]
    const = lambda bi, i: (0, 0)
    return pl.pallas_call(
        _cross_body,
        grid=(b, s // tq),
        in_specs=[pl.BlockSpec((1, tq, d), lambda bi, i: (bi, i, 0)), pl.BlockSpec((1, d), const),
                  pl.BlockSpec(wq.shape, const),
                  pl.BlockSpec((1, nm, cw), lambda bi, i: (bi, 0, 0)),
                  pl.BlockSpec((1, nm, cw), lambda bi, i: (bi, 0, 0)),
                  pl.BlockSpec(wo.shape, const), pl.BlockSpec((1, d), const)],
        out_specs=pl.BlockSpec((1, tq, d), lambda bi, i: (bi, i, 0)),
        out_shape=jax.ShapeDtypeStruct((b, s, d), F32),
        compiler_params=_params("parallel", "parallel"),
        name="cross",
    )(x3, gpre, wq, km, vm, wo, gpost)


def _ffn_body(x_ref, xh_ref, gpre_ref, wg_ref, wu_ref, cw_ref, cb_ref, wd_ref, gpost_ref, o_ref,
              xn_ref, act_ref, *, seq_len):
    i = pl.program_id(0)
    tm = x_ref.shape[0]
    g = gpre_ref[...]
    x = x_ref[...]
    xn_ref[8:, :] = _rms(x, g).astype(BF16)
    starts_seq = (i * tm) % seq_len == 0
    xn_ref[0:8, :] = jnp.where(starts_seq, 0.0, _rms(xh_ref[...], g)).astype(BF16)
    xe = xn_ref[...]
    for c in range(0, act_ref.shape[1], FFN_CHUNK):
        cs = slice(c, c + FFN_CHUNK)
        ge = _dot(xe, wg_ref[:, cs])
        up = _dot(xe[8:], wu_ref[:, cs])
        cw = cw_ref[:, cs]
        gate = cw[0:1] * ge[6:6 + tm] + cw[1:2] * ge[7:7 + tm] + cw[2:3] * ge[8:8 + tm] + cb_ref[:, cs]
        act_ref[:, cs] = (jax.nn.silu(gate) * up).astype(BF16)
    o_ref[...] = x + _rms(_dot(act_ref[...], wd_ref[...]), gpost_ref[...])


def _ffn(x2, gpre, wg, wu, cw, cb, wd, gpost, seq_len, tm):
    m, d = x2.shape
    f = wg.shape[1]
    const = lambda i: (0, 0)
    resident = lambda shape: pl.BlockSpec(shape, const, pipeline_mode=pl.Buffered(1))
    return pl.pallas_call(
        functools.partial(_ffn_body, seq_len=seq_len),
        grid=(m // tm,),
        in_specs=[pl.BlockSpec((tm, d), lambda i: (i, 0)),
                  pl.BlockSpec((8, d), lambda i: (jnp.maximum(i * (tm // 8) - 1, 0), 0)),
                  pl.BlockSpec((1, d), const),
                  resident((d, f)), resident((d, f)),
                  pl.BlockSpec((3, f), const), pl.BlockSpec((1, f), const),
                  resident((f, d)),
                  pl.BlockSpec((1, d), const)],
        out_specs=pl.BlockSpec((tm, d), lambda i: (i, 0)),
        out_shape=jax.ShapeDtypeStruct((m, d), F32),
        scratch_shapes=[pltpu.VMEM((tm + 8, d), BF16), pltpu.VMEM((tm, f), BF16)],
        compiler_params=_params("parallel"),
        name="ffn",
    )(x2, x2, gpre, wg, wu, cw, cb, wd, gpost)


def _row_tile(m, pref):
    t = pref
    while m % t:
        t //= 2
    return t


def kernel(x, mem, w_in, b_in, mlstm_norm, sconv_w, rel_bias, hgrn_lb_logits, hgrn_norm, w_mix_out, norm_mix_pre, norm_mix_post, mem_norm, w_cq, w_ck, w_cv, w_co, norm_cross_pre, norm_cross_post, w_ffn_in, ffn_conv_w, ffn_conv_b, w_ffn_out, norm_ffn_pre, norm_ffn_post):
    b, s, d = x.shape
    depth = w_in.shape[0]
    d_ff = w_ffn_out.shape[1]
    assert s % MOBA_BLOCK == 0 and s % MLSTM_CHUNK == 0 and s % HGRN_CHUNK == 0
    m = b * s
    tm = _row_tile(m, 512)

    gate0 = 4 * GW
    hq0 = gate0 + 2 * HEADS + 6 * GW
    main_cols = lambda a: jnp.concatenate([a[..., :gate0], a[..., gate0 + 2 * HEADS:hq0 + GW], a[..., hq0 + 2 * GW:]], axis=-1)
    aux_cols = lambda a: jnp.concatenate(
        [a[..., hq0 + GW:hq0 + 2 * GW], a[..., gate0:gate0 + 2 * HEADS],
         jnp.zeros(a.shape[:-1] + (N_AUX - GW - 2 * HEADS,), a.dtype)], axis=-1)
    wm_all = main_cols(w_in).astype(BF16)
    wa_all = aux_cols(w_in).astype(BF16)
    bm_all = main_cols(b_in)[:, None, :]
    ba_all = aux_cols(b_in)[:, None, :]

    w_out = w_mix_out.astype(BF16)
    wq, wk, wv, wo = (w.astype(BF16) for w in (w_cq, w_ck, w_cv, w_co))
    w_gate = w_ffn_in[..., :d_ff].astype(BF16)
    w_up = w_ffn_in[..., d_ff:].astype(BF16)
    w_down = w_ffn_out.astype(BF16)
    row = lambda a, l: a[l][None, :]

    buckets, far_bucket = _bucket_tables()
    bias_tiles = _bias_tiles(rel_bias, buckets)
    assert d_ff % FFN_CHUNK == 0

    x2 = x.reshape(m, d)
    for l in range(depth):
        main, aux = _inproj(x2, row(norm_mix_pre, l), wm_all[l], bm_all[l], wa_all[l], ba_all[l], tm)
        main3 = main.reshape(b, s, N_MAIN)
        aux3 = aux.reshape(b, s, N_AUX)
        y_m = _mlstm(main3, aux3, row(mlstm_norm, l))
        y_c = _sconv(main3, sconv_w[l])
        y_a = _moba(main3, rel_bias, bias_tiles, far_bucket)
        y_h = _hgrn(main3, aux3, hgrn_lb_logits, row(hgrn_norm, l), l)
        ys = [y.reshape(m, GW) for y in (y_m, y_c, y_a, y_h)]
        x2 = _outproj(ys, w_out[l], row(norm_mix_post, l), x2, tm)

        km, vm = _memkv(mem, row(mem_norm, l), wk[l], wv[l])
        x2 = _cross(x2.reshape(b, s, d), row(norm_cross_pre, l), wq[l], km, vm, wo[l],
                    row(norm_cross_post, l), _row_tile(s, 512)).reshape(m, d)

        x2 = _ffn(x2, row(norm_ffn_pre, l), w_gate[l], w_up[l], ffn_conv_w[l], row(ffn_conv_b, l),
                  w_down[l], row(norm_ffn_post, l), s, tm)
    return x2.reshape(b, s, d)
```

```python
import functools
import math

import numpy as np
import jax
import jax.numpy as jnp
from jax import lax
from jax.experimental import pallas as pl
from jax.experimental.pallas import tpu as pltpu

F32 = jnp.float32
BF16 = jnp.bfloat16

HEADS = 4
HEAD_DIM = 64
GW = HEADS * HEAD_DIM
MOBA_BLOCK = 256
MOBA_TOPK = 3
REL_BUCKETS = 32
REL_MAX_DIST = 128
CROSS_HEADS = 4
CROSS_HEAD_DIM = 128
RMS_EPS = 1e-6
NEG_BIG = -1e30

MLSTM_CHUNK = 256
MLSTM_BATCH = 1
HGRN_CHUNK = 64
FFN_CHUNK = 256
VMEM_LIMIT = 56 * 1024 * 1024

(CB_MQ, CB_MK, CB_MV, CB_MO, CB_CB, CB_CC, CB_CH, CB_AQ, CB_AK, CB_AV,
 CB_HQ, CB_HI, CB_HG) = range(13)
N_MAIN = 13 * GW
N_AUX = 512


def _rms(x, g):
    return x * lax.rsqrt(jnp.mean(x * x, axis=-1, keepdims=True) + RMS_EPS) * g


def _nt(a, b):
    return lax.dot_general(a, b, (((1,), (1,)), ((), ())), preferred_element_type=F32)


def _tn(a, b):
    return lax.dot_general(a, b, (((0,), (0,)), ((), ())), preferred_element_type=F32)


def _dot(a, b):
    return jnp.dot(a, b, preferred_element_type=F32)


def _head_of(shape, axis):
    return lax.shift_right_logical(lax.broadcasted_iota(jnp.int32, shape, axis), 6)


def _expand_heads(cols, width=GW):
    rows = cols[0].shape[0]
    hid = _head_of((rows, width), 1)
    return jnp.where(hid == 0, cols[0], jnp.where(hid == 1, cols[1], jnp.where(hid == 2, cols[2], cols[3])))


def _params(*sem):
    return pltpu.CompilerParams(dimension_semantics=sem, vmem_limit_bytes=VMEM_LIMIT)


def _inproj_body(x_ref, g_ref, wm_ref, bm_ref, wa_ref, ba_ref, om_ref, oa_ref):
    xn = _rms(x_ref[...], g_ref[...]).astype(BF16)
    for c in range(0, om_ref.shape[-1], GW):
        acc = _dot(xn, wm_ref[:, c:c + GW]) + bm_ref[:, c:c + GW]
        om_ref[:, c:c + GW] = acc.astype(om_ref.dtype)
    oa_ref[...] = _dot(xn, wa_ref[...]) + ba_ref[...]


def _inproj(x2, gain, wm, bm, wa, ba, tm):
    m, d = x2.shape
    const = lambda i: (0, 0)
    return pl.pallas_call(
        _inproj_body,
        grid=(m // tm,),
        in_specs=[
            pl.BlockSpec((tm, d), lambda i: (i, 0)),
            pl.BlockSpec((1, d), const),
            pl.BlockSpec((d, N_MAIN), const),
            pl.BlockSpec((1, N_MAIN), const),
            pl.BlockSpec((d, N_AUX), const),
            pl.BlockSpec((1, N_AUX), const),
        ],
        out_specs=[
            pl.BlockSpec((tm, N_MAIN), lambda i: (i, 0)),
            pl.BlockSpec((tm, N_AUX), lambda i: (i, 0)),
        ],
        out_shape=[
            jax.ShapeDtypeStruct((m, N_MAIN), BF16),
            jax.ShapeDtypeStruct((m, N_AUX), F32),
        ],
        compiler_params=_params("parallel"),
        name="inproj",
    )(x2, gain, wm, bm, wa, ba)


def _sconv_body(b_ref, c_ref, h_ref, w_ref, o_ref):
    u = c_ref[0].astype(F32) * h_ref[0].astype(F32)
    row = lax.broadcasted_iota(jnp.int32, u.shape, 0)
    u1 = jnp.where(row >= 1, pltpu.roll(u, 1, 0), 0.0)
    u2 = jnp.where(row >= 2, pltpu.roll(u, 2, 0), 0.0)
    w = w_ref[...]
    y = w[0:1] * u2 + w[1:2] * u1 + w[2:3] * u
    o_ref[0] = (b_ref[0].astype(F32) * y).astype(o_ref.dtype)


def _sconv(main3, w):
    b, s, _ = main3.shape
    col = lambda cb: pl.BlockSpec((1, s, GW), lambda i: (i, 0, cb))
    return pl.pallas_call(
        _sconv_body,
        grid=(b,),
        in_specs=[col(CB_CB), col(CB_CC), col(CB_CH), pl.BlockSpec(w.shape, lambda i: (0, 0))],
        out_specs=pl.BlockSpec((1, s, GW), lambda i: (i, 0, 0)),
        out_shape=jax.ShapeDtypeStruct((b, s, GW), BF16),
        compiler_params=_params("parallel"),
        name="sconv",
    )(main3, main3, main3, w)


def _scan_lanes(x, op, seg):
    pos = lax.broadcasted_iota(jnp.int32, x.shape, 1) & (seg - 1)
    s = 1
    while s < seg:
        x = jnp.where(pos >= s, op(x, pltpu.roll(x, s, 1)), x)
        s *= 2
    return x


def _head_rms_gate(hval, gain, gate, bones):
    sq = hval * hval
    hi = sq.astype(BF16)
    lo = (sq - hi.astype(F32)).astype(BF16)
    ms = (_dot(hi, bones) + _dot(lo, bones)) * (1.0 / HEAD_DIM)
    return hval * lax.rsqrt(ms + RMS_EPS) * gain * gate


def _block_ones():
    r = _head_of((GW, GW), 0)
    c = _head_of((GW, GW), 1)
    return r == c


def _split3(x):
    hi = x.astype(BF16).astype(F32)
    r = x - hi
    mid = r.astype(BF16).astype(F32)
    lo = (r - mid).astype(BF16).astype(F32)
    return hi, mid, lo


def _mlstm_body(q_ref, k_ref, v_ref, o_ref, g_ref, gain_ref, y_ref, ct_ref, n_ref, m_ref, g_ref2, c_ref2, cm_ref2):
    s_len = q_ref.shape[1]
    L = MLSTM_CHUNK
    iota = lambda shape, axis: lax.broadcasted_iota(jnp.int32, shape, axis)
    bd = _block_ones()
    bones = jnp.where(bd, 1.0, 0.0).astype(BF16)
    hid_l = _head_of((L, GW), 1)
    hmask = [hid_l == h for h in range(HEADS)]
    tril = iota((L, L), 0) >= iota((L, L), 1)
    row8 = iota((8, L), 0)
    row16 = iota((16, L), 0)
    hid1 = _head_of((1, GW), 1)
    gain = gain_ref[...]
    ones_bd = jnp.where(_head_of((HEADS * L, GW), 1) == iota((HEADS * L, GW), 0) // L, 1.0, 0.0).astype(BF16)
    csel = iota((128, 3 * GW), 1)
    esel = jnp.where(iota((128, 3 * GW), 0) == 16 + 4 * (csel // GW) + (csel % GW) // HEAD_DIM, 1.0, 0.0).astype(BF16)
    ones8 = jnp.ones((8, L), BF16)

    ct_ref[...] = jnp.zeros_like(ct_ref)
    n_ref[...] = jnp.zeros_like(n_ref)
    m_ref[...] = jnp.zeros_like(m_ref)

    for bb in range(q_ref.shape[0]):
        gt = jnp.concatenate([jnp.transpose(g_ref[bb, c * L:(c + 1) * L, :])[0:8] for c in range(s_len // L)], axis=1)
        g = _scan_lanes(jax.nn.log_sigmoid(pltpu.roll(gt, 4, 0)), jnp.add, L)
        cc = gt - g
        g_ref2[bb] = g
        c_ref2[bb] = cc
        cm_ref2[bb] = _scan_lanes(cc, jnp.maximum, L)

    def one_batch(bb, r0):
        q = q_ref[bb, pl.ds(r0, L), :]
        k = (k_ref[bb, pl.ds(r0, L), :].astype(F32) * (HEAD_DIM ** -0.5)).astype(BF16)
        v = v_ref[bb, pl.ds(r0, L), :]
        g = g_ref2[bb, :, pl.ds(r0, L)]
        cc = c_ref2[bb, :, pl.ds(r0, L)]
        mprev = m_ref[bb][:, 0:1]
        pp = jnp.maximum(mprev, cm_ref2[bb, :, pl.ds(r0, L)])
        ein = jnp.exp(mprev - pp)
        p_last = pp[:, L - 1:L]
        ws = jnp.exp(cc - p_last)
        decay = jnp.exp(mprev - p_last)
        m_new = g[:, L - 1:L] + p_last
        emt = jnp.exp(-(g + pp))
        p_hi, p_mid, p_lo = _split3(pp)
        c_hi, c_mid, c_lo = _split3(cc)

        b0 = jnp.where(row8 < 4, p_hi, pltpu.roll(p_mid, 4, 0))
        b1 = jnp.where(row8 < 4, p_lo, jnp.where(row8 < 7, 1.0, 0.0))
        b2 = jnp.where(row8 < 4, ein, pltpu.roll(ws, 4, 0))
        b3 = jnp.where(row8 < 4, emt, 0.0)
        cols = jnp.transpose(jnp.concatenate([b0, b1, b2, b3, jnp.zeros((96, L), F32)], axis=0)).astype(BF16)

        ws_list = []
        for h in range(HEADS):
            pick = (row16 == h) | (row16 == 4 + h) | (row16 == 8 + h)
            bh = jnp.where(pick, -1.0, jnp.where(row16 == 12, c_hi[h:h + 1], jnp.where(
                row16 == 13, c_mid[h:h + 1], jnp.where(row16 == 14, c_lo[h:h + 1], 0.0))))
            bh = jnp.concatenate([bh, jnp.zeros((112, L), F32)], axis=0).astype(BF16)
            expo = _dot(cols, bh)
            sc = _nt(jnp.where(hmask[h], q, jnp.zeros_like(q)), k)
            ws_list.append((jnp.exp(jnp.where(tril, expo, NEG_BIG)) * sc).astype(BF16))
        w_cat = jnp.concatenate(ws_list, axis=1)
        v_bd = jnp.concatenate([jnp.where(hmask[h], v, jnp.zeros_like(v)) for h in range(HEADS)], axis=0)
        num = _dot(w_cat, v_bd)
        den = _dot(w_cat, ones_bd)
        x3 = _dot(cols, esel)
        ein_x, ws_x, emt_x = x3[:, :GW], x3[:, GW:2 * GW], x3[:, 2 * GW:]

        ct = ct_ref[bb]
        nrow = n_ref[bb][0:1]
        nbd = jnp.where(bd, jnp.broadcast_to(nrow, (GW, GW)), 0.0).astype(BF16)
        num = num + ein_x * _nt(q, ct.astype(BF16))
        den = den + ein_x * _nt(q, nbd)
        hval = num / jnp.maximum(jnp.abs(den), emt_x)
        gate = jax.nn.sigmoid(o_ref[bb, pl.ds(r0, L), :].astype(F32))
        y_ref[bb, pl.ds(r0, L), :] = _head_rms_gate(hval, gain, gate, bones).astype(y_ref.dtype)

        dec_l = jnp.where(hid1 == 0, decay[0:1], jnp.where(hid1 == 1, decay[1:2], jnp.where(hid1 == 2, decay[2:3], decay[3:4])))
        vw = (v.astype(F32) * ws_x).astype(BF16)
        kw = (k.astype(F32) * ws_x).astype(BF16)
        ct_ref[bb] = ct * dec_l + jnp.where(bd, _tn(vw, k), 0.0)
        n_ref[bb] = jnp.broadcast_to(nrow * dec_l + _dot(ones8, kw)[0:1], n_ref.shape[1:])
        m_ref[bb] = jnp.broadcast_to(m_new, m_ref.shape[1:])

    def step(c, carry):
        r0 = pl.multiple_of(c * L, L)
        for bb in range(q_ref.shape[0]):
            one_batch(bb, r0)
        return carry

    lax.fori_loop(0, s_len // L, step, 0)


def _mlstm(main3, aux3, gain):
    b, s, _ = main3.shape
    nb = MLSTM_BATCH if b % MLSTM_BATCH == 0 else 1
    col = lambda cb: pl.BlockSpec((nb, s, GW), lambda i: (i, 0, cb))
    return pl.pallas_call(
        _mlstm_body,
        grid=(b // nb,),
        in_specs=[col(CB_MQ), col(CB_MK), col(CB_MV), col(CB_MO),
                  pl.BlockSpec((nb, s, 128), lambda i: (i, 0, 2)),
                  pl.BlockSpec((1, GW), lambda i: (0, 0))],
        out_specs=pl.BlockSpec((nb, s, GW), lambda i: (i, 0, 0)),
        out_shape=jax.ShapeDtypeStruct((b, s, GW), BF16),
        scratch_shapes=[pltpu.VMEM((nb, GW, GW), F32), pltpu.VMEM((nb, 8, GW), F32), pltpu.VMEM((nb, 8, 128), F32)]
        + [pltpu.VMEM((nb, 8, s), F32)] * 3,
        compiler_params=_params("parallel"),
        name="mlstm",
    )(main3, main3, main3, main3, aux3, gain)


def _seg_prefix(x, seg, pos):
    s = 1
    while s < seg:
        x = x + jnp.where(pos >= s, pltpu.roll(x, s, 0), 0.0)
        s *= 2
    return x


def _hgrn_body(lbl_ref, q_ref, i_ref, g_ref, f_ref, gain_ref, y_ref, st_ref, *, layer):
    s_len = q_ref.shape[1]
    C = HGRN_CHUNK
    bd = _block_ones()
    bones = jnp.where(bd, 1.0, 0.0).astype(BF16)
    hid_c = _head_of((C, GW), 1)
    hmask = [hid_c == h for h in range(HEADS)]
    gain = gain_ref[...]

    lg = lbl_ref[...]
    ex = jnp.exp(lg - jnp.max(lg, axis=0, keepdims=True))
    soft = ex / jnp.sum(ex, axis=0, keepdims=True)
    lb = jnp.zeros((1, GW), F32)
    for j in range(1, layer + 1):
        lb = lb + soft[j:j + 1]

    tq = lax.broadcasted_iota(jnp.int32, (HEADS * C, C), 0) & (C - 1)
    tk = lax.broadcasted_iota(jnp.int32, (HEADS * C, C), 1)
    levels = []
    bsz = C
    while bsz >= 2:
        half = bsz // 2
        same = (tq // bsz) == (tk // bsz)
        levels.append((bsz, same & ((tq % bsz) >= half) & ((tk % bsz) < half)))
        bsz = half
    eye = tq == tk
    rowpos = lax.broadcasted_iota(jnp.int32, (C, GW), 0)

    st_ref[...] = jnp.zeros_like(st_ref)

    level_mask = dict(levels)
    hm_b = [jnp.where(hmask[h], 1.0, 0.0).astype(BF16) for h in range(HEADS)]

    def stack_heads(a):
        ab = a.astype(BF16)
        return jnp.concatenate([ab * hm_b[h] for h in range(HEADS)], axis=0)

    def step(c, carry):
        r0 = pl.multiple_of(c * C, C)
        z = f_ref[0, pl.ds(r0, C), :]
        f = lb + (1.0 - lb) * jax.nn.sigmoid(z)
        lf = jnp.log(f)
        kk = 1.0 - f
        qq = jax.nn.silu(q_ref[0, pl.ds(r0, C), :].astype(F32))
        vv = i_ref[0, pl.ds(r0, C), :]

        gcum = _seg_prefix(lf, C, rowpos)
        st = st_ref[...]
        out = _nt((qq * jnp.exp(gcum)).astype(BF16), st.astype(BF16))

        att = jnp.where(eye, _nt(stack_heads(qq), kk.astype(BF16)), 0.0)
        ge = gcum
        half = 1
        while half < C:
            qd = qq * jnp.exp(jnp.minimum(gcum - pltpu.roll(ge, half, 0), 0.0))
            kd = kk * jnp.exp(ge - gcum)
            att = jnp.where(level_mask[2 * half], _nt(stack_heads(qd), kd.astype(BF16)), att)
            ge = jnp.where((rowpos & half) != 0, ge, pltpu.roll(ge, C - half, 0))
            half *= 2
        for h in range(HEADS):
            out = out + jnp.where(hmask[h], _dot(att[h * C:(h + 1) * C].astype(BF16), vv), 0.0)

        gate = jax.nn.silu(g_ref[0, pl.ds(r0, C), :].astype(F32))
        y_ref[0, pl.ds(r0, C), :] = _head_rms_gate(out, gain, gate, bones).astype(y_ref.dtype)

        g_last = gcum[C - 1:C, :]
        kdec = (kk * jnp.exp(g_last - gcum)).astype(BF16)
        st_ref[...] = st * jnp.exp(g_last) + jnp.where(bd, _tn(vv, kdec), 0.0)
        return carry

    lax.fori_loop(0, s_len // C, step, 0)


def _hgrn(main3, aux3, lb_logits, gain, layer):
    b, s, _ = main3.shape
    col = lambda cb: pl.BlockSpec((1, s, GW), lambda i: (i, 0, cb))
    return pl.pallas_call(
        functools.partial(_hgrn_body, layer=layer),
        grid=(b,),
        in_specs=[pl.BlockSpec(lb_logits.shape, lambda i: (0, 0)),
                  col(CB_HQ), col(CB_HI), col(CB_HG),
                  pl.BlockSpec((1, s, GW), lambda i: (i, 0, 0)),
                  pl.BlockSpec((1, GW), lambda i: (0, 0))],
        out_specs=pl.BlockSpec((1, s, GW), lambda i: (i, 0, 0)),
        out_shape=jax.ShapeDtypeStruct((b, s, GW), BF16),
        scratch_shapes=[pltpu.VMEM((GW, GW), F32)],
        compiler_params=_params("parallel"),
        name="hgrn2",
    )(lb_logits, main3, main3, main3, aux3, gain)


def _bucket_tables():
    def bucket(dist):
        n = np.maximum(dist, 0)
        exact = REL_BUCKETS // 2
        nf = np.maximum(n, 1).astype(np.float32)
        large = exact + (np.log(nf / np.float32(exact)) / np.float32(math.log(REL_MAX_DIST / exact))
                         * np.float32(REL_BUCKETS - exact)).astype(np.int32)
        large = np.minimum(large, REL_BUCKETS - 1)
        return np.where(n < exact, n, large).astype(np.int32)
    tk = np.arange(MOBA_BLOCK)[:, None]
    tq = np.arange(MOBA_BLOCK)[None, :]
    own = np.where(tq - tk >= 0, bucket(tq - tk), REL_BUCKETS)
    adj = bucket(MOBA_BLOCK + tq - tk)
    far = int(bucket(np.array([2 * MOBA_BLOCK]))[0])
    return np.stack([own, adj]).astype(np.int32), far


def _bias_body(rb_ref, bk_ref, o_ref):
    bk = bk_ref[...]
    bs = bk.shape[-1]
    for h in range(HEADS):
        acc = jnp.full(bk.shape, NEG_BIG, F32)
        for j in range(REL_BUCKETS):
            acc = jnp.where(bk == j, rb_ref[j, h], acc)
        o_ref[:, :, h * bs:(h + 1) * bs] = acc


def _bias_tiles(rel_bias, buckets):
    two, bk, bq = buckets.shape
    return pl.pallas_call(
        _bias_body,
        in_specs=[pl.BlockSpec(memory_space=pltpu.SMEM), pl.BlockSpec(buckets.shape, lambda: (0, 0, 0))],
        out_specs=pl.BlockSpec((two, bk, HEADS * bq), lambda: (0, 0, 0)),
        out_shape=jax.ShapeDtypeStruct((two, bk, HEADS * bq), F32),
        name="moba_bias",
    )(rel_bias, jnp.asarray(buckets))


VT_ROWS = HEAD_DIM + 16


def _moba_body(rb_ref, q_ref, k_ref, v_ref, bias_ref, y_ref, vt_ref, km_ref, qbd_ref, *, far_bucket):
    i = pl.program_id(1)
    BS = MOBA_BLOCK
    nb = k_ref.shape[1] // BS
    n_sel = min(MOBA_TOPK, nb - 1)
    scale = HEAD_DIM ** -0.5

    @pl.when(i == 0)
    def _():
        hid = _head_of((1, GW), 1)
        vt_ref[:, HEAD_DIM:, :] = jnp.ones((HEADS, VT_ROWS - HEAD_DIM, nb * BS), BF16)
        for n in range(nb):
            cols = slice(n * BS, (n + 1) * BS)
            vt = jnp.transpose(v_ref[0, cols, :].astype(F32)).astype(BF16)
            km = jnp.mean(k_ref[0, cols, :].astype(F32), axis=0, keepdims=True)
            for h in range(HEADS):
                vt_ref[h, 0:HEAD_DIM, cols] = vt[h * HEAD_DIM:(h + 1) * HEAD_DIM]
                km_ref[h * nb + n:h * nb + n + 1, :] = jnp.where(hid == h, km, 0.0).astype(BF16)

    q = q_ref[0]
    hid_q = _head_of((BS, GW), 1)
    qs = (q.astype(F32) * scale).astype(BF16)
    for h in range(HEADS):
        qbd_ref[h * BS:(h + 1) * BS, :] = jnp.where(hid_q == h, qs, jnp.zeros_like(qs))

    gate = _nt(km_ref[...], q)
    blk = lax.broadcasted_iota(jnp.int32, (nb, BS), 0)
    sel = []
    for h in range(HEADS):
        g = jnp.where(blk < i, gate[h * nb:(h + 1) * nb], NEG_BIG)
        rank = jnp.zeros((nb, BS), jnp.int32)
        for mrow in range(nb):
            gm = g[mrow:mrow + 1]
            ahead = (gm > g) | ((gm == g) & (mrow < blk))
            rank = rank + jnp.where(ahead, 1, 0)
        chosen = (rank < n_sel) & (blk < i)
        far = jnp.where(blk < i - 1, rb_ref[far_bucket, h], 0.0)
        sel.append(jnp.where(chosen, far, NEG_BIG))
    sel = jnp.concatenate(sel, axis=1)

    def attend(n_past):
        nblk = n_past + 1
        s_all = _nt(k_ref[0, 0:nblk * BS, :], qbd_ref[...])
        blocks = []
        for n in range(nblk):
            s = s_all[n * BS:(n + 1) * BS]
            if n == n_past:
                s = s + bias_ref[0]
            elif n == n_past - 1:
                s = s + (bias_ref[1] + sel[n:n + 1])
            else:
                s = s + sel[n:n + 1]
            blocks.append(s)
        mx = jnp.max(blocks[n_past], axis=0, keepdims=True)
        for n in range(n_past):
            mx = jnp.maximum(mx, jnp.max(blocks[n], axis=0, keepdims=True))
        acc = [jnp.zeros((VT_ROWS, BS), F32) for _ in range(HEADS)]
        for n in range(nblk):
            p = jnp.exp(blocks[n] - mx).astype(BF16)
            for h in range(HEADS):
                acc[h] = acc[h] + _dot(vt_ref[h, :, n * BS:(n + 1) * BS], p[:, h * BS:(h + 1) * BS])
        out_t = jnp.concatenate([a[0:HEAD_DIM] * (1.0 / a[HEAD_DIM:HEAD_DIM + 1]) for a in acc], axis=0)
        y_ref[0] = jnp.transpose(out_t).astype(y_ref.dtype)

    for n_past in range(nb):
        pl.when(i == n_past)(functools.partial(attend, n_past))


def _moba(main3, rel_bias, bias_tiles, far_bucket):
    b, s, _ = main3.shape
    nb = s // MOBA_BLOCK
    return pl.pallas_call(
        functools.partial(_moba_body, far_bucket=far_bucket),
        grid=(b, nb),
        in_specs=[pl.BlockSpec(memory_space=pltpu.SMEM),
                  pl.BlockSpec((1, MOBA_BLOCK, GW), lambda bi, i: (bi, i, CB_AQ)),
                  pl.BlockSpec((1, s, GW), lambda bi, i: (bi, 0, CB_AK)),
                  pl.BlockSpec((1, s, GW), lambda bi, i: (bi, 0, CB_AV)),
                  pl.BlockSpec(bias_tiles.shape, lambda bi, i: (0, 0, 0), pipeline_mode=pl.Buffered(1))],
        out_specs=pl.BlockSpec((1, MOBA_BLOCK, GW), lambda bi, i: (bi, i, 0)),
        out_shape=jax.ShapeDtypeStruct((b, s, GW), BF16),
        scratch_shapes=[pltpu.VMEM((HEADS, VT_ROWS, s), BF16),
                        pltpu.VMEM((HEADS * nb, GW), BF16),
                        pltpu.VMEM((HEADS * MOBA_BLOCK, GW), BF16)],
        compiler_params=_params("parallel", "arbitrary"),
        name="moba",
    )(rel_bias, main3, main3, main3, bias_tiles)


def _outproj_body(ym_ref, yc_ref, ya_ref, yh_ref, w_ref, g_ref, x_ref, o_ref):
    h = (_dot(ym_ref[...], w_ref[0 * GW:1 * GW, :]) + _dot(yc_ref[...], w_ref[1 * GW:2 * GW, :])
         + _dot(ya_ref[...], w_ref[2 * GW:3 * GW, :]) + _dot(yh_ref[...], w_ref[3 * GW:4 * GW, :]))
    o_ref[...] = x_ref[...] + _rms(h, g_ref[...])


def _outproj(ys, w, gain, x2, tm):
    m, d = x2.shape
    row = lambda width: pl.BlockSpec((tm, width), lambda i: (i, 0))
    const = lambda i: (0, 0)
    return pl.pallas_call(
        _outproj_body,
        grid=(m // tm,),
        in_specs=[row(GW), row(GW), row(GW), row(GW),
                  pl.BlockSpec(w.shape, const), pl.BlockSpec((1, d), const), row(d)],
        out_specs=row(d),
        out_shape=jax.ShapeDtypeStruct((m, d), F32),
        compiler_params=_params("parallel"),
        name="outproj",
    )(*ys, w, gain, x2)


def _memkv_body(mem_ref, g_ref, wk_ref, wv_ref, k_ref, v_ref):
    mn = _rms(mem_ref[0], g_ref[...]).astype(BF16)
    k_ref[0] = _dot(mn, wk_ref[...]).astype(BF16)
    v_ref[0] = _dot(mn, wv_ref[...]).astype(BF16)


def _memkv(mem, gain, wk, wv):
    b, nm, d = mem.shape
    cw = wk.shape[1]
    const = lambda i: (0, 0)
    return pl.pallas_call(
        _memkv_body,
        grid=(b,),
        in_specs=[pl.BlockSpec((1, nm, d), lambda i: (i, 0, 0)), pl.BlockSpec((1, d), const),
                  pl.BlockSpec(wk.shape, const), pl.BlockSpec(wv.shape, const)],
        out_specs=[pl.BlockSpec((1, nm, cw), lambda i: (i, 0, 0))] * 2,
        out_shape=[jax.ShapeDtypeStruct((b, nm, cw), BF16)] * 2,
        compiler_params=_params("parallel"),
        name="memkv",
    )(mem, gain, wk, wv)


def _cross_body(x_ref, gpre_ref, wq_ref, k_ref, v_ref, wo_ref, gpost_ref, o_ref):
    x = x_ref[0]
    xn = _rms(x, gpre_ref[...]).astype(BF16)
    q = _dot(xn, wq_ref[...]).astype(BF16)
    k = k_ref[0]
    v = v_ref[0]
    outs = []
    for h in range(CROSS_HEADS):
        sl = slice(h * CROSS_HEAD_DIM, (h + 1) * CROSS_HEAD_DIM)
        s = _nt(q[:, sl], k[:, sl]) * (CROSS_HEAD_DIM ** -0.5)
        p = jnp.exp(s - jnp.max(s, axis=-1, keepdims=True))
        p = p / jnp.sum(p, axis=-1, keepdims=True)
        outs.append(_dot(p.astype(BF16), v[:, sl]).astype(BF16))
    o = jnp.concatenate(outs, axis=-1)
    o_ref[0] = x + _rms(_dot(o, wo_ref[...]), gpost_ref[...])


def _cross(x3, gpre, wq, km, vm, wo, gpost, tq):
    b, s, d = x3.shape
    nm, cw = km.shape[1:]
    const = lambda bi, i: (0, 0)
    return pl.pallas_call(
        _cross_body,
        grid=(b, s // tq),
        in_specs=[pl.BlockSpec((1, tq, d), lambda bi, i: (bi, i, 0)), pl.BlockSpec((1, d), const),
                  pl.BlockSpec(wq.shape, const),
                  pl.BlockSpec((1, nm, cw), lambda bi, i: (bi, 0, 0)),
                  pl.BlockSpec((1, nm, cw), lambda bi, i: (bi, 0, 0)),
                  pl.BlockSpec(wo.shape, const), pl.BlockSpec((1, d), const)],
        out_specs=pl.BlockSpec((1, tq, d), lambda bi, i: (bi, i, 0)),
        out_shape=jax.ShapeDtypeStruct((b, s, d), F32),
        compiler_params=_params("parallel", "parallel"),
        name="cross",
    )(x3, gpre, wq, km, vm, wo, gpost)


def _ffn_body(x_ref, xh_ref, gpre_ref, wg_ref, wu_ref, cw_ref, cb_ref, wd_ref, gpost_ref, o_ref,
              xn_ref, act_ref, *, seq_len):
    i = pl.program_id(0)
    tm = x_ref.shape[0]
    g = gpre_ref[...]
    x = x_ref[...]
    xn_ref[8:, :] = _rms(x, g).astype(BF16)
    starts_seq = (i * tm) % seq_len == 0
    xn_ref[0:8, :] = jnp.where(starts_seq, 0.0, _rms(xh_ref[...], g)).astype(BF16)
    xe = xn_ref[...]
    for c in range(0, act_ref.shape[1], FFN_CHUNK):
        cs = slice(c, c + FFN_CHUNK)
        ge = _dot(xe, wg_ref[:, cs])
        up = _dot(xe[8:], wu_ref[:, cs])
        cw = cw_ref[:, cs]
        gate = cw[0:1] * ge[6:6 + tm] + cw[1:2] * ge[7:7 + tm] + cw[2:3] * ge[8:8 + tm] + cb_ref[:, cs]
        act_ref[:, cs] = (jax.nn.silu(gate) * up).astype(BF16)
    o_ref[...] = x + _rms(_dot(act_ref[...], wd_ref[...]), gpost_ref[...])


def _ffn(x2, gpre, wg, wu, cw, cb, wd, gpost, seq_len, tm):
    m, d = x2.shape
    f = wg.shape[1]
    const = lambda i: (0, 0)
    resident = lambda shape: pl.BlockSpec(shape, const, pipeline_mode=pl.Buffered(1))
    return pl.pallas_call(
        functools.partial(_ffn_body, seq_len=seq_len),
        grid=(m // tm,),
        in_specs=[pl.BlockSpec((tm, d), lambda i: (i, 0)),
                  pl.BlockSpec((8, d), lambda i: (jnp.maximum(i * (tm // 8) - 1, 0), 0)),
                  pl.BlockSpec((1, d), const),
                  resident((d, f)), resident((d, f)),
                  pl.BlockSpec((3, f), const), pl.BlockSpec((1, f), const),
                  resident((f, d)),
                  pl.BlockSpec((1, d), const)],
        out_specs=pl.BlockSpec((tm, d), lambda i: (i, 0)),
        out_shape=jax.ShapeDtypeStruct((m, d), F32),
        scratch_shapes=[pltpu.VMEM((tm + 8, d), BF16), pltpu.VMEM((tm, f), BF16)],
        compiler_params=_params("parallel"),
        name="ffn",
    )(x2, x2, gpre, wg, wu, cw, cb, wd, gpost)


def _row_tile(m, pref):
    t = pref
    while m % t:
        t //= 2
    return t


def kernel(x, mem, w_in, b_in, mlstm_norm, sconv_w, rel_bias, hgrn_lb_logits, hgrn_norm, w_mix_out, norm_mix_pre, norm_mix_post, mem_norm, w_cq, w_ck, w_cv, w_co, norm_cross_pre, norm_cross_post, w_ffn_in, ffn_conv_w, ffn_conv_b, w_ffn_out, norm_ffn_pre, norm_ffn_post):
    b, s, d = x.shape
    depth = w_in.shape[0]
    d_ff = w_ffn_out.shape[1]
    assert s % MOBA_BLOCK == 0 and s % MLSTM_CHUNK == 0 and s % HGRN_CHUNK == 0
    m = b * s
    tm = _row_tile(m, 512)

    gate0 = 4 * GW
    hq0 = gate0 + 2 * HEADS + 6 * GW
    main_cols = lambda a: jnp.concatenate([a[..., :gate0], a[..., gate0 + 2 * HEADS:hq0 + GW], a[..., hq0 + 2 * GW:]], axis=-1)
    aux_cols = lambda a: jnp.concatenate(
        [a[..., hq0 + GW:hq0 + 2 * GW], a[..., gate0:gate0 + 2 * HEADS],
         jnp.zeros(a.shape[:-1] + (N_AUX - GW - 2 * HEADS,), a.dtype)], axis=-1)
    wm_all = main_cols(w_in).astype(BF16)
    wa_all = aux_cols(w_in).astype(BF16)
    bm_all = main_cols(b_in)[:, None, :]
    ba_all = aux_cols(b_in)[:, None, :]

    w_out = w_mix_out.astype(BF16)
    wq, wk, wv, wo = (w.astype(BF16) for w in (w_cq, w_ck, w_cv, w_co))
    w_gate = w_ffn_in[..., :d_ff].astype(BF16)
    w_up = w_ffn_in[..., d_ff:].astype(BF16)
    w_down = w_ffn_out.astype(BF16)
    row = lambda a, l: a[l][None, :]

    buckets, far_bucket = _bucket_tables()
    bias_tiles = _bias_tiles(rel_bias, buckets)
    assert d_ff % FFN_CHUNK == 0

    x2 = x.reshape(m, d)
    for l in range(depth):
        main, aux = _inproj(x2, row(norm_mix_pre, l), wm_all[l], bm_all[l], wa_all[l], ba_all[l], tm)
        main3 = main.reshape(b, s, N_MAIN)
        aux3 = aux.reshape(b, s, N_AUX)
        y_m = _mlstm(main3, aux3, row(mlstm_norm, l))
        y_c = _sconv(main3, sconv_w[l])
        y_a = _moba(main3, rel_bias, bias_tiles, far_bucket)
        y_h = _hgrn(main3, aux3, hgrn_lb_logits, row(hgrn_norm, l), l)
        ys = [y.reshape(m, GW) for y in (y_m, y_c, y_a, y_h)]
        x2 = _outproj(ys, w_out[l], row(norm_mix_post, l), x2, tm)

        km, vm = _memkv(mem, row(mem_norm, l), wk[l], wv[l])
        x2 = _cross(x2.reshape(b, s, d), row(norm_cross_pre, l), wq[l], km, vm, wo[l],
                    row(norm_cross_post, l), _row_tile(s, 512)).reshape(m, d)

        x2 = _ffn(x2, row(norm_ffn_pre, l), w_gate[l], w_up[l], ffn_conv_w[l], row(ffn_conv_b, l),
                  w_down[l], row(norm_ffn_post, l), s, tm)
    return x2.reshape(b, s, d)
```

```python
import functools
import math

import numpy as np
import jax
import jax.numpy as jnp
from jax import lax
from jax.experimental import pallas as pl
from jax.experimental.pallas import tpu as pltpu

F32 = jnp.float32
BF16 = jnp.bfloat16

HEADS = 4
HEAD_DIM = 64
GW = HEADS * HEAD_DIM
MOBA_BLOCK = 256
MOBA_TOPK = 3
REL_BUCKETS = 32
REL_MAX_DIST = 128
CROSS_HEADS = 4
CROSS_HEAD_DIM = 128
RMS_EPS = 1e-6
NEG_BIG = -1e30
LOG2E = 1.4426950408889634

MLSTM_CHUNK = 256
MLSTM_BATCH = 1
HGRN_CHUNK = 64
HGRN_BATCH = 2
FFN_CHUNK = 256
VMEM_LIMIT = 56 * 1024 * 1024

(CB_MQ, CB_MK, CB_MV, CB_MO, CB_CB, CB_CC, CB_CH, CB_AQ, CB_AK, CB_AV,
 CB_HQ, CB_HI, CB_HG) = range(13)
N_MAIN = 13 * GW
N_AUX = 512


def _rms(x, g):
    return x * lax.rsqrt(jnp.mean(x * x, axis=-1, keepdims=True) + RMS_EPS) * g


def _nt(a, b):
    return lax.dot_general(a, b, (((1,), (1,)), ((), ())), preferred_element_type=F32)


def _tn(a, b):
    return lax.dot_general(a, b, (((0,), (0,)), ((), ())), preferred_element_type=F32)


def _dot(a, b):
    return jnp.dot(a, b, preferred_element_type=F32)


def _head_of(shape, axis):
    return lax.shift_right_logical(lax.broadcasted_iota(jnp.int32, shape, axis), 6)


def _expand_heads(cols, width=GW):
    rows = cols[0].shape[0]
    hid = _head_of((rows, width), 1)
    return jnp.where(hid == 0, cols[0], jnp.where(hid == 1, cols[1], jnp.where(hid == 2, cols[2], cols[3])))


def _params(*sem):
    return pltpu.CompilerParams(dimension_semantics=sem, vmem_limit_bytes=VMEM_LIMIT)


def _inproj_body(x_ref, g_ref, wm_ref, bm_ref, wa_ref, ba_ref, om_ref, oa_ref):
    xn = _rms(x_ref[...], g_ref[...]).astype(BF16)
    for c in range(0, om_ref.shape[-1], GW):
        acc = _dot(xn, wm_ref[:, c:c + GW]) + bm_ref[:, c:c + GW]
        om_ref[:, c:c + GW] = acc.astype(om_ref.dtype)
    oa_ref[...] = _dot(xn, wa_ref[...]) + ba_ref[...]


def _inproj(x2, gain, wm, bm, wa, ba, tm):
    m, d = x2.shape
    const = lambda i: (0, 0)
    return pl.pallas_call(
        _inproj_body,
        grid=(m // tm,),
        in_specs=[
            pl.BlockSpec((tm, d), lambda i: (i, 0)),
            pl.BlockSpec((1, d), const),
            pl.BlockSpec((d, N_MAIN), const),
            pl.BlockSpec((1, N_MAIN), const),
            pl.BlockSpec((d, N_AUX), const),
            pl.BlockSpec((1, N_AUX), const),
        ],
        out_specs=[
            pl.BlockSpec((tm, N_MAIN), lambda i: (i, 0)),
            pl.BlockSpec((tm, N_AUX), lambda i: (i, 0)),
        ],
        out_shape=[
            jax.ShapeDtypeStruct((m, N_MAIN), BF16),
            jax.ShapeDtypeStruct((m, N_AUX), F32),
        ],
        compiler_params=_params("parallel"),
        name="inproj",
    )(x2, gain, wm, bm, wa, ba)


def _sconv_body(b_ref, c_ref, h_ref, w_ref, o_ref):
    u = c_ref[0].astype(F32) * h_ref[0].astype(F32)
    row = lax.broadcasted_iota(jnp.int32, u.shape, 0)
    u1 = jnp.where(row >= 1, pltpu.roll(u, 1, 0), 0.0)
    u2 = jnp.where(row >= 2, pltpu.roll(u, 2, 0), 0.0)
    w = w_ref[...]
    y = w[0:1] * u2 + w[1:2] * u1 + w[2:3] * u
    o_ref[0] = (b_ref[0].astype(F32) * y).astype(o_ref.dtype)


def _sconv(main3, w):
    b, s, _ = main3.shape
    col = lambda cb: pl.BlockSpec((1, s, GW), lambda i: (i, 0, cb))
    return pl.pallas_call(
        _sconv_body,
        grid=(b,),
        in_specs=[col(CB_CB), col(CB_CC), col(CB_CH), pl.BlockSpec(w.shape, lambda i: (0, 0))],
        out_specs=pl.BlockSpec((1, s, GW), lambda i: (i, 0, 0)),
        out_shape=jax.ShapeDtypeStruct((b, s, GW), BF16),
        compiler_params=_params("parallel"),
        name="sconv",
    )(main3, main3, main3, w)


def _scan_lanes(x, op, seg):
    pos = lax.broadcasted_iota(jnp.int32, x.shape, 1) & (seg - 1)
    s = 1
    while s < seg:
        x = jnp.where(pos >= s, op(x, pltpu.roll(x, s, 1)), x)
        s *= 2
    return x


def _head_rms_gate(hval, gain, gate, bones):
    sq = hval * hval
    hi = sq.astype(BF16)
    lo = (sq - hi.astype(F32)).astype(BF16)
    ms = (_dot(hi, bones) + _dot(lo, bones)) * (1.0 / HEAD_DIM)
    return hval * lax.rsqrt(ms + RMS_EPS) * gain * gate


def _block_ones():
    r = _head_of((GW, GW), 0)
    c = _head_of((GW, GW), 1)
    return r == c


def _split3(x):
    hi = x.astype(BF16).astype(F32)
    r = x - hi
    mid = r.astype(BF16).astype(F32)
    lo = (r - mid).astype(BF16).astype(F32)
    return hi, mid, lo


def _mlstm_body(q_ref, k_ref, v_ref, o_ref, g_ref, gain_ref, y_ref, ct_ref, n_ref, m_ref, g_ref2, c_ref2, cm_ref2):
    s_len = q_ref.shape[1]
    L = MLSTM_CHUNK
    iota = lambda shape, axis: lax.broadcasted_iota(jnp.int32, shape, axis)
    bd = _block_ones()
    bones = jnp.where(bd, 1.0, 0.0).astype(BF16)
    hid_l = _head_of((L, GW), 1)
    hmask = [hid_l == h for h in range(HEADS)]
    tril = iota((L, L), 0) >= iota((L, L), 1)
    row8 = iota((8, L), 0)
    row16 = iota((16, L), 0)
    hid1 = _head_of((1, GW), 1)
    gain = gain_ref[...]
    ones_bd = jnp.where(_head_of((HEADS * L, GW), 1) == iota((HEADS * L, GW), 0) // L, 1.0, 0.0).astype(BF16)
    csel = iota((128, 3 * GW), 1)
    esel = jnp.where(iota((128, 3 * GW), 0) == 16 + 4 * (csel // GW) + (csel % GW) // HEAD_DIM, 1.0, 0.0).astype(BF16)
    ones8 = jnp.ones((8, L), BF16)

    ct_ref[...] = jnp.zeros_like(ct_ref)
    n_ref[...] = jnp.zeros_like(n_ref)
    m_ref[...] = jnp.zeros_like(m_ref)

    for bb in range(q_ref.shape[0]):
        gt = jnp.concatenate([jnp.transpose(g_ref[bb, c * L:(c + 1) * L, :])[0:8] for c in range(s_len // L)], axis=1)
        g = _scan_lanes(jax.nn.log_sigmoid(pltpu.roll(gt, 4, 0)), jnp.add, L)
        cc = gt - g
        g_ref2[bb] = g
        c_ref2[bb] = cc
        cm_ref2[bb] = _scan_lanes(cc, jnp.maximum, L)

    def one_batch(bb, r0):
        q = q_ref[bb, pl.ds(r0, L), :]
        k = (k_ref[bb, pl.ds(r0, L), :].astype(F32) * (HEAD_DIM ** -0.5)).astype(BF16)
        v = v_ref[bb, pl.ds(r0, L), :]
        g = g_ref2[bb, :, pl.ds(r0, L)]
        cc = c_ref2[bb, :, pl.ds(r0, L)]
        mprev = m_ref[bb][:, 0:1]
        pp = jnp.maximum(mprev, cm_ref2[bb, :, pl.ds(r0, L)])
        ein = jnp.exp(mprev - pp)
        p_last = pp[:, L - 1:L]
        ws = jnp.exp(cc - p_last)
        decay = jnp.exp(mprev - p_last)
        m_new = g[:, L - 1:L] + p_last
        emt = jnp.exp(-(g + pp))
        p_hi, p_mid, p_lo = _split3(pp)
        c_hi, c_mid, c_lo = _split3(cc)

        b0 = jnp.where(row8 < 4, p_hi, pltpu.roll(p_mid, 4, 0))
        b1 = jnp.where(row8 < 4, p_lo, jnp.where(row8 < 7, 1.0, 0.0))
        b2 = jnp.where(row8 < 4, ein, pltpu.roll(ws, 4, 0))
        b3 = jnp.where(row8 < 4, emt, 0.0)
        cols = jnp.transpose(jnp.concatenate([b0, b1, b2, b3, jnp.zeros((96, L), F32)], axis=0)).astype(BF16)

        ws_list = []
        for h in range(HEADS):
            pick = (row16 == h) | (row16 == 4 + h) | (row16 == 8 + h)
            bh = jnp.where(pick, -1.0, jnp.where(row16 == 12, c_hi[h:h + 1], jnp.where(
                row16 == 13, c_mid[h:h + 1], jnp.where(row16 == 14, c_lo[h:h + 1], 0.0))))
            bh = jnp.concatenate([bh, jnp.zeros((112, L), F32)], axis=0).astype(BF16)
            expo = _dot(cols, bh)
            sc = _nt(jnp.where(hmask[h], q, jnp.zeros_like(q)), k)
            ws_list.append((jnp.exp(jnp.where(tril, expo, NEG_BIG)) * sc).astype(BF16))
        w_cat = jnp.concatenate(ws_list, axis=1)
        v_bd = jnp.concatenate([jnp.where(hmask[h], v, jnp.zeros_like(v)) for h in range(HEADS)], axis=0)
        num = _dot(w_cat, v_bd)
        den = _dot(w_cat, ones_bd)
        x3 = _dot(cols, esel)
        ein_x, ws_x, emt_x = x3[:, :GW], x3[:, GW:2 * GW], x3[:, 2 * GW:]

        ct = ct_ref[bb]
        nrow = n_ref[bb][0:1]
        nbd = jnp.where(bd, jnp.broadcast_to(nrow, (GW, GW)), 0.0).astype(BF16)
        num = num + ein_x * _nt(q, ct.astype(BF16))
        den = den + ein_x * _nt(q, nbd)
        hval = num / jnp.maximum(jnp.abs(den), emt_x)
        gate = jax.nn.sigmoid(o_ref[bb, pl.ds(r0, L), :].astype(F32))
        y_ref[bb, pl.ds(r0, L), :] = _head_rms_gate(hval, gain, gate, bones).astype(y_ref.dtype)

        dec_l = jnp.where(hid1 == 0, decay[0:1], jnp.where(hid1 == 1, decay[1:2], jnp.where(hid1 == 2, decay[2:3], decay[3:4])))
        vw = (v.astype(F32) * ws_x).astype(BF16)
        kw = (k.astype(F32) * ws_x).astype(BF16)
        ct_ref[bb] = ct * dec_l + jnp.where(bd, _tn(vw, k), 0.0)
        n_ref[bb] = jnp.broadcast_to(nrow * dec_l + _dot(ones8, kw)[0:1], n_ref.shape[1:])
        m_ref[bb] = jnp.broadcast_to(m_new, m_ref.shape[1:])

    def step(c, carry):
        r0 = pl.multiple_of(c * L, L)
        for bb in range(q_ref.shape[0]):
            one_batch(bb, r0)
        return carry

    lax.fori_loop(0, s_len // L, step, 0)


def _mlstm(main3, aux3, gain):
    b, s, _ = main3.shape
    nb = MLSTM_BATCH if b % MLSTM_BATCH == 0 else 1
    col = lambda cb: pl.BlockSpec((nb, s, GW), lambda i: (i, 0, cb))
    return pl.pallas_call(
        _mlstm_body,
        grid=(b // nb,),
        in_specs=[col(CB_MQ), col(CB_MK), col(CB_MV), col(CB_MO),
                  pl.BlockSpec((nb, s, 128), lambda i: (i, 0, 2)),
                  pl.BlockSpec((1, GW), lambda i: (0, 0))],
        out_specs=pl.BlockSpec((nb, s, GW), lambda i: (i, 0, 0)),
        out_shape=jax.ShapeDtypeStruct((b, s, GW), BF16),
        scratch_shapes=[pltpu.VMEM((nb, GW, GW), F32), pltpu.VMEM((nb, 8, GW), F32), pltpu.VMEM((nb, 8, 128), F32)]
        + [pltpu.VMEM((nb, 8, s), F32)] * 3,
        compiler_params=_params("parallel"),
        name="mlstm",
    )(main3, main3, main3, main3, aux3, gain)


def _seg_prefix(x, seg, pos):
    s = 1
    while s < seg:
        x = x + jnp.where(pos >= s, pltpu.roll(x, s, 0), 0.0)
        s *= 2
    return x


def _hgrn_body(lbl_ref, q_ref, i_ref, g_ref, f_ref, gain_ref, y_ref, st_ref, *, layer):
    nbat, s_len = q_ref.shape[0], q_ref.shape[1]
    C = HGRN_CHUNK
    R = nbat * C
    bd = _block_ones()
    bones = jnp.where(bd, 1.0, 0.0).astype(BF16)
    hid_r = _head_of((R, GW), 1)
    hm_b = [jnp.where(hid_r == h, 1.0, 0.0).astype(BF16) for h in range(HEADS)]
    hid_c = _head_of((C, GW), 1)
    hmask = [hid_c == h for h in range(HEADS)]
    gain = gain_ref[...]

    lg = lbl_ref[...]
    ex = jnp.exp(lg - jnp.max(lg, axis=0, keepdims=True))
    soft = ex / jnp.sum(ex, axis=0, keepdims=True)
    lb = jnp.zeros((1, GW), F32)
    for j in range(1, layer + 1):
        lb = lb + soft[j:j + 1]

    tq = lax.broadcasted_iota(jnp.int32, (HEADS * C, C), 0) & (C - 1)
    tk = lax.broadcasted_iota(jnp.int32, (HEADS * C, C), 1)
    level_mask = {}
    bsz = C
    while bsz >= 2:
        half = bsz // 2
        same = (tq // bsz) == (tk // bsz)
        level_mask[bsz] = same & ((tq % bsz) >= half) & ((tk % bsz) < half)
        bsz = half
    eye = tq == tk
    rowpos = lax.broadcasted_iota(jnp.int32, (R, GW), 0) & (C - 1)

    st_ref[...] = jnp.zeros_like(st_ref)

    def load(ref, r0):
        return ref[:, pl.ds(r0, C), :].reshape(R, GW)

    def stack_heads(a, bb):
        return jnp.concatenate([(a * hm_b[h])[bb * C:(bb + 1) * C] for h in range(HEADS)], axis=0)

    def step(c, carry):
        r0 = pl.multiple_of(c * C, C)
        f = lb + (1.0 - lb) * jax.nn.sigmoid(load(f_ref, r0))
        lf = jnp.log(f)
        kk = 1.0 - f
        qq = jax.nn.silu(load(q_ref, r0).astype(F32))
        vv = load(i_ref, r0)
        rows = [slice(bb * C, (bb + 1) * C) for bb in range(nbat)]

        gcum = _seg_prefix(lf, C, rowpos)
        st = [st_ref[bb] for bb in range(nbat)]
        qin = (qq * jnp.exp(gcum)).astype(BF16)
        out = [_nt(qin[rows[bb]], st[bb].astype(BF16)) for bb in range(nbat)]

        qb, kb = qq.astype(BF16), kk.astype(BF16)
        att = [jnp.where(eye, _nt(stack_heads(qb, bb), kb[rows[bb]]), 0.0) for bb in range(nbat)]
        ge = gcum
        half = 1
        while half < C:
            qd = (qq * jnp.exp(jnp.minimum(gcum - pltpu.roll(ge, half, 0), 0.0))).astype(BF16)
            kd = (kk * jnp.exp(ge - gcum)).astype(BF16)
            att = [jnp.where(level_mask[2 * half], _nt(stack_heads(qd, bb), kd[rows[bb]]), att[bb]) for bb in range(nbat)]
            ge = jnp.where((rowpos & half) != 0, ge, pltpu.roll(ge, R - half, 0))
            half *= 2
        for bb in range(nbat):
            ab = att[bb].astype(BF16)
            for h in range(HEADS):
                out[bb] = out[bb] + jnp.where(hmask[h], _dot(ab[h * C:(h + 1) * C], vv[rows[bb]]), 0.0)

        gate = jax.nn.silu(load(g_ref, r0).astype(F32))
        y = _head_rms_gate(jnp.concatenate(out, axis=0), gain, gate, bones).astype(y_ref.dtype)
        y_ref[:, pl.ds(r0, C), :] = y.reshape(nbat, C, GW)

        for bb in range(nbat):
            g_last = gcum[bb * C + C - 1:bb * C + C, :]
            kdec = (kk[rows[bb]] * jnp.exp(g_last - gcum[rows[bb]])).astype(BF16)
            st_ref[bb] = st[bb] * jnp.exp(g_last) + jnp.where(bd, _tn(vv[rows[bb]], kdec), 0.0)
        return carry

    lax.fori_loop(0, s_len // C, step, 0)


def _hgrn(main3, aux3, lb_logits, gain, layer):
    b, s, _ = main3.shape
    nbat = HGRN_BATCH if b % HGRN_BATCH == 0 else 1
    col = lambda cb: pl.BlockSpec((nbat, s, GW), lambda i: (i, 0, cb))
    return pl.pallas_call(
        functools.partial(_hgrn_body, layer=layer),
        grid=(b // nbat,),
        in_specs=[pl.BlockSpec(lb_logits.shape, lambda i: (0, 0)),
                  col(CB_HQ), col(CB_HI), col(CB_HG),
                  pl.BlockSpec((nbat, s, GW), lambda i: (i, 0, 0)),
                  pl.BlockSpec((1, GW), lambda i: (0, 0))],
        out_specs=pl.BlockSpec((nbat, s, GW), lambda i: (i, 0, 0)),
        out_shape=jax.ShapeDtypeStruct((b, s, GW), BF16),
        scratch_shapes=[pltpu.VMEM((nbat, GW, GW), F32)],
        compiler_params=_params("parallel"),
        name="hgrn2",
    )(lb_logits, main3, main3, main3, aux3, gain)


def _bucket_tables():
    def bucket(dist):
        n = np.maximum(dist, 0)
        exact = REL_BUCKETS // 2
        nf = np.maximum(n, 1).astype(np.float32)
        large = exact + (np.log(nf / np.float32(exact)) / np.float32(math.log(REL_MAX_DIST / exact))
                         * np.float32(REL_BUCKETS - exact)).astype(np.int32)
        large = np.minimum(large, REL_BUCKETS - 1)
        return np.where(n < exact, n, large).astype(np.int32)
    tk = np.arange(MOBA_BLOCK)[:, None]
    tq = np.arange(MOBA_BLOCK)[None, :]
    own = np.where(tq - tk >= 0, bucket(tq - tk), REL_BUCKETS)
    adj = bucket(MOBA_BLOCK + tq - tk)
    far = int(bucket(np.array([2 * MOBA_BLOCK]))[0])
    return np.stack([own, adj]).astype(np.int32), far


def _bias_body(rb_ref, bk_ref, o_ref):
    bk = bk_ref[...]
    bs = bk.shape[-1]
    for h in range(HEADS):
        acc = jnp.full(bk.shape, NEG_BIG, F32)
        for j in range(REL_BUCKETS):
            acc = jnp.where(bk == j, rb_ref[j, h] * LOG2E, acc)
        o_ref[:, :, h * bs:(h + 1) * bs] = acc


def _bias_tiles(rel_bias, buckets):
    two, bk, bq = buckets.shape
    return pl.pallas_call(
        _bias_body,
        in_specs=[pl.BlockSpec(memory_space=pltpu.SMEM), pl.BlockSpec(buckets.shape, lambda: (0, 0, 0))],
        out_specs=pl.BlockSpec((two, bk, HEADS * bq), lambda: (0, 0, 0)),
        out_shape=jax.ShapeDtypeStruct((two, bk, HEADS * bq), F32),
        name="moba_bias",
    )(rel_bias, jnp.asarray(buckets))


VT_ROWS = HEAD_DIM + 16


def _moba_body(rb_ref, q_ref, k_ref, v_ref, bias_ref, y_ref, vt_ref, km_ref, qbd_ref, *, far_bucket):
    i = pl.program_id(1)
    BS = MOBA_BLOCK
    nb = k_ref.shape[1] // BS
    n_sel = min(MOBA_TOPK, nb - 1)
    scale = HEAD_DIM ** -0.5

    @pl.when(i == 0)
    def _():
        hid = _head_of((1, GW), 1)
        vt_ref[:, HEAD_DIM:, :] = jnp.ones((HEADS, VT_ROWS - HEAD_DIM, nb * BS), BF16)
        for n in range(nb):
            cols = slice(n * BS, (n + 1) * BS)
            vt = jnp.transpose(v_ref[0, cols, :].astype(F32)).astype(BF16)
            km = jnp.mean(k_ref[0, cols, :].astype(F32), axis=0, keepdims=True)
            for h in range(HEADS):
                vt_ref[h, 0:HEAD_DIM, cols] = vt[h * HEAD_DIM:(h + 1) * HEAD_DIM]
                km_ref[h * nb + n:h * nb + n + 1, :] = jnp.where(hid == h, km, 0.0).astype(BF16)

    q = q_ref[0]
    hid_q = _head_of((BS, GW), 1)
    qs = (q.astype(F32) * (scale * LOG2E)).astype(BF16)
    for h in range(HEADS):
        qbd_ref[h * BS:(h + 1) * BS, :] = jnp.where(hid_q == h, qs, jnp.zeros_like(qs))

    gate = _nt(km_ref[...], q)
    blk = lax.broadcasted_iota(jnp.int32, (nb, BS), 0)
    sel = []
    for h in range(HEADS):
        g = jnp.where(blk < i, gate[h * nb:(h + 1) * nb], NEG_BIG)
        rank = jnp.zeros((nb, BS), jnp.int32)
        for mrow in range(nb):
            gm = g[mrow:mrow + 1]
            ahead = (gm > g) | ((gm == g) & (mrow < blk))
            rank = rank + jnp.where(ahead, 1, 0)
        chosen = (rank < n_sel) & (blk < i)
        far = jnp.where(blk < i - 1, rb_ref[far_bucket, h] * LOG2E, 0.0)
        sel.append(jnp.where(chosen, far, NEG_BIG))
    sel = jnp.concatenate(sel, axis=1)

    def attend(n_past):
        nblk = n_past + 1
        keys = k_ref[0, 0:nblk * BS, :]
        qk = lambda h: _nt(keys, qbd_ref[h * BS:(h + 1) * BS, :])
        s_next = qk(0)
        outs = []
        for h in range(HEADS):
            s_all = s_next
            if h + 1 < HEADS:
                s_next = qk(h + 1)
            hs = slice(h * BS, (h + 1) * BS)
            m = acc = None
            for n in range(nblk):
                s = s_all[n * BS:(n + 1) * BS]
                if n == n_past:
                    s, r = s + bias_ref[0, :, hs], None
                elif n == n_past - 1:
                    s, r = s + bias_ref[1, :, hs], sel[n:n + 1, hs]
                else:
                    r = sel[n:n + 1, hs]
                bm = jnp.max(s, axis=0, keepdims=True)
                bm = bm if r is None else bm + r
                m_new = bm if m is None else jnp.maximum(m, bm)
                p = jnp.exp2(s - (m_new if r is None else m_new - r)).astype(BF16)
                pv = _dot(vt_ref[h, :, n * BS:(n + 1) * BS], p)
                acc = pv if m is None else jnp.exp2(m - m_new) * acc + pv
                m = m_new
            outs.append(acc[0:HEAD_DIM] * (1.0 / acc[HEAD_DIM:HEAD_DIM + 1]))
        y_ref[0] = jnp.transpose(jnp.concatenate(outs, axis=0)).astype(y_ref.dtype)

    for n_past in range(nb):
        pl.when(i == n_past)(functools.partial(attend, n_past))


def _moba(main3, rel_bias, bias_tiles, far_bucket):
    b, s, _ = main3.shape
    nb = s // MOBA_BLOCK
    return pl.pallas_call(
        functools.partial(_moba_body, far_bucket=far_bucket),
        grid=(b, nb),
        in_specs=[pl.BlockSpec(memory_space=pltpu.SMEM),
                  pl.BlockSpec((1, MOBA_BLOCK, GW), lambda bi, i: (bi, i, CB_AQ)),
                  pl.BlockSpec((1, s, GW), lambda bi, i: (bi, 0, CB_AK)),
                  pl.BlockSpec((1, s, GW), lambda bi, i: (bi, 0, CB_AV)),
                  pl.BlockSpec(bias_tiles.shape, lambda bi, i: (0, 0, 0), pipeline_mode=pl.Buffered(1))],
        out_specs=pl.BlockSpec((1, MOBA_BLOCK, GW), lambda bi, i: (bi, i, 0)),
        out_shape=jax.ShapeDtypeStruct((b, s, GW), BF16),
        scratch_shapes=[pltpu.VMEM((HEADS, VT_ROWS, s), BF16),
                        pltpu.VMEM((HEADS * nb, GW), BF16),
                        pltpu.VMEM((HEADS * MOBA_BLOCK, GW), BF16)],
        compiler_params=_params("parallel", "arbitrary"),
        name="moba",
    )(rel_bias, main3, main3, main3, bias_tiles)


def _memkv_body(mem_ref, g_ref, wk_ref, wv_ref, k_ref, v_ref):
    mn = _rms(mem_ref[0], g_ref[...]).astype(BF16)
    k_ref[0] = _dot(mn, wk_ref[...]).astype(BF16)
    v_ref[0] = _dot(mn, wv_ref[...]).astype(BF16)


def _memkv(mem, gain, wk, wv):
    b, nm, d = mem.shape
    cw = wk.shape[1]
    const = lambda i: (0, 0)
    return pl.pallas_call(
        _memkv_body,
        grid=(b,),
        in_specs=[pl.BlockSpec((1, nm, d), lambda i: (i, 0, 0)), pl.BlockSpec((1, d), const),
                  pl.BlockSpec(wk.shape, const), pl.BlockSpec(wv.shape, const)],
        out_specs=[pl.BlockSpec((1, nm, cw), lambda i: (i, 0, 0))] * 2,
        out_shape=[jax.ShapeDtypeStruct((b, nm, cw), BF16)] * 2,
        compiler_params=_params("parallel"),
        name="memkv",
    )(mem, gain, wk, wv)


def _mixout_cross_body(ym_ref, yc_ref, ya_ref, yh_ref, wout_ref, gmix_ref, x_ref,
                       gpre_ref, wq_ref, k_ref, v_ref, wo_ref, gpost_ref, o_ref):
    h = (_dot(ym_ref[0], wout_ref[0 * GW:1 * GW, :]) + _dot(yc_ref[0], wout_ref[1 * GW:2 * GW, :])
         + _dot(ya_ref[0], wout_ref[2 * GW:3 * GW, :]) + _dot(yh_ref[0], wout_ref[3 * GW:4 * GW, :]))
    x = x_ref[0] + _rms(h, gmix_ref[...])
    xn = _rms(x, gpre_ref[...]).astype(BF16)
    q = _dot(xn, wq_ref[...]).astype(BF16)
    k = k_ref[0]
    v = v_ref[0]
    outs = []
    for hh in range(CROSS_HEADS):
        sl = slice(hh * CROSS_HEAD_DIM, (hh + 1) * CROSS_HEAD_DIM)
        s = _nt(q[:, sl], k[:, sl]) * (CROSS_HEAD_DIM ** -0.5)
        p = jnp.exp(s - jnp.max(s, axis=-1, keepdims=True))
        inv = 1.0 / jnp.sum(p, axis=-1, keepdims=True)
        outs.append((_dot(p.astype(BF16), v[:, sl]) * inv).astype(BF16))
    o = jnp.concatenate(outs, axis=-1)
    o_ref[0] = x + _rms(_dot(o, wo_ref[...]), gpost_ref[...])


def _mixout_cross(ys, wout, gmix, x3, gpre, wq, km, vm, wo, gpost, tq):
    b, s, d = x3.shape
    nm, cw = km.shape[1:]
    const = lambda bi, i: (0, 0)
    tile = lambda width: pl.BlockSpec((1, tq, width), lambda bi, i: (bi, i, 0))
    mem = pl.BlockSpec((1, nm, cw), lambda bi, i: (bi, 0, 0))
    return pl.pallas_call(
        _mixout_cross_body,
        grid=(b, s // tq),
        in_specs=[tile(GW), tile(GW), tile(GW), tile(GW),
                  pl.BlockSpec(wout.shape, const), pl.BlockSpec((1, d), const), tile(d),
                  pl.BlockSpec((1, d), const), pl.BlockSpec(wq.shape, const), mem, mem,
                  pl.BlockSpec(wo.shape, const), pl.BlockSpec((1, d), const)],
        out_specs=tile(d),
        out_shape=jax.ShapeDtypeStruct((b, s, d), F32),
        compiler_params=_params("parallel", "parallel"),
        name="mixout_cross",
    )(*ys, wout, gmix, x3, gpre, wq, km, vm, wo, gpost)


def _ffn_body(x_ref, xh_ref, gpre_ref, wg_ref, wu_ref, cw_ref, cb_ref, wd_ref, gpost_ref, o_ref,
              xn_ref, act_ref, *, seq_len):
    i = pl.program_id(0)
    tm = x_ref.shape[0]
    g = gpre_ref[...]
    x = x_ref[...]
    xn_ref[8:, :] = _rms(x, g).astype(BF16)
    starts_seq = (i * tm) % seq_len == 0
    xn_ref[0:8, :] = jnp.where(starts_seq, 0.0, _rms(xh_ref[...], g)).astype(BF16)
    xe = xn_ref[...]
    for c in range(0, act_ref.shape[1], FFN_CHUNK):
        cs = slice(c, c + FFN_CHUNK)
        ge = _dot(xe, wg_ref[:, cs])
        up = _dot(xe[8:], wu_ref[:, cs])
        cw = cw_ref[:, cs]
        gate = cw[0:1] * ge[6:6 + tm] + cw[1:2] * ge[7:7 + tm] + cw[2:3] * ge[8:8 + tm] + cb_ref[:, cs]
        act_ref[:, cs] = (jax.nn.silu(gate) * up).astype(BF16)
    o_ref[...] = x + _rms(_dot(act_ref[...], wd_ref[...]), gpost_ref[...])


def _ffn(x2, gpre, wg, wu, cw, cb, wd, gpost, seq_len, tm):
    m, d = x2.shape
    f = wg.shape[1]
    const = lambda i: (0, 0)
    resident = lambda shape: pl.BlockSpec(shape, const, pipeline_mode=pl.Buffered(1))
    return pl.pallas_call(
        functools.partial(_ffn_body, seq_len=seq_len),
        grid=(m // tm,),
        in_specs=[pl.BlockSpec((tm, d), lambda i: (i, 0)),
                  pl.BlockSpec((8, d), lambda i: (jnp.maximum(i * (tm // 8) - 1, 0), 0)),
                  pl.BlockSpec((1, d), const),
                  resident((d, f)), resident((d, f)),
                  pl.BlockSpec((3, f), const), pl.BlockSpec((1, f), const),
                  resident((f, d)),
                  pl.BlockSpec((1, d), const)],
        out_specs=pl.BlockSpec((tm, d), lambda i: (i, 0)),
        out_shape=jax.ShapeDtypeStruct((m, d), F32),
        scratch_shapes=[pltpu.VMEM((tm + 8, d), BF16), pltpu.VMEM((tm, f), BF16)],
        compiler_params=_params("parallel"),
        name="ffn",
    )(x2, x2, gpre, wg, wu, cw, cb, wd, gpost)


def _row_tile(m, pref):
    t = pref
    while m % t:
        t //= 2
    return t


def kernel(x, mem, w_in, b_in, mlstm_norm, sconv_w, rel_bias, hgrn_lb_logits, hgrn_norm, w_mix_out, norm_mix_pre, norm_mix_post, mem_norm, w_cq, w_ck, w_cv, w_co, norm_cross_pre, norm_cross_post, w_ffn_in, ffn_conv_w, ffn_conv_b, w_ffn_out, norm_ffn_pre, norm_ffn_post):
    b, s, d = x.shape
    depth = w_in.shape[0]
    d_ff = w_ffn_out.shape[1]
    assert s % MOBA_BLOCK == 0 and s % MLSTM_CHUNK == 0 and s % HGRN_CHUNK == 0
    m = b * s
    tm = _row_tile(m, 512)

    gate0 = 4 * GW
    hq0 = gate0 + 2 * HEADS + 6 * GW
    main_cols = lambda a: jnp.concatenate([a[..., :gate0], a[..., gate0 + 2 * HEADS:hq0 + GW], a[..., hq0 + 2 * GW:]], axis=-1)
    aux_cols = lambda a: jnp.concatenate(
        [a[..., hq0 + GW:hq0 + 2 * GW], a[..., gate0:gate0 + 2 * HEADS],
         jnp.zeros(a.shape[:-1] + (N_AUX - GW - 2 * HEADS,), a.dtype)], axis=-1)
    wm_all = main_cols(w_in).astype(BF16)
    wa_all = aux_cols(w_in).astype(BF16)
    bm_all = main_cols(b_in)[:, None, :]
    ba_all = aux_cols(b_in)[:, None, :]

    w_out = w_mix_out.astype(BF16)
    wq, wk, wv, wo = (w.astype(BF16) for w in (w_cq, w_ck, w_cv, w_co))
    w_gate = w_ffn_in[..., :d_ff].astype(BF16)
    w_up = w_ffn_in[..., d_ff:].astype(BF16)
    w_down = w_ffn_out.astype(BF16)
    row = lambda a, l: a[l][None, :]

    buckets, far_bucket = _bucket_tables()
    bias_tiles = _bias_tiles(rel_bias, buckets)
    assert d_ff % FFN_CHUNK == 0

    x2 = x.reshape(m, d)
    for l in range(depth):
        main, aux = _inproj(x2, row(norm_mix_pre, l), wm_all[l], bm_all[l], wa_all[l], ba_all[l], tm)
        main3 = main.reshape(b, s, N_MAIN)
        aux3 = aux.reshape(b, s, N_AUX)
        y_m = _mlstm(main3, aux3, row(mlstm_norm, l))
        y_c = _sconv(main3, sconv_w[l])
        y_a = _moba(main3, rel_bias, bias_tiles, far_bucket)
        y_h = _hgrn(main3, aux3, hgrn_lb_logits, row(hgrn_norm, l), l)
        km, vm = _memkv(mem, row(mem_norm, l), wk[l], wv[l])
        x2 = _mixout_cross((y_m, y_c, y_a, y_h), w_out[l], row(norm_mix_post, l), x2.reshape(b, s, d),
                           row(norm_cross_pre, l), wq[l], km, vm, wo[l], row(norm_cross_post, l),
                           _row_tile(s, 512)).reshape(m, d)

        x2 = _ffn(x2, row(norm_ffn_pre, l), w_gate[l], w_up[l], ffn_conv_w[l], row(ffn_conv_b, l),
                  w_down[l], row(norm_ffn_post, l), s, tm)
    return x2.reshape(b, s, d)
```

```python
import functools
import math

import numpy as np
import jax
import jax.numpy as jnp
from jax import lax
from jax.experimental import pallas as pl
from jax.experimental.pallas import tpu as pltpu

F32 = jnp.float32
BF16 = jnp.bfloat16

HEADS = 4
HEAD_DIM = 64
GW = HEADS * HEAD_DIM
MOBA_BLOCK = 256
MOBA_TOPK = 3
REL_BUCKETS = 32
REL_MAX_DIST = 128
CROSS_HEADS = 4
CROSS_HEAD_DIM = 128
RMS_EPS = 1e-6
NEG_BIG = -1e30
LOG2E = 1.4426950408889634

MLSTM_CHUNK = 256
MLSTM_BATCH = 1
HGRN_CHUNK = 64
HGRN_BATCH = 2
FFN_CHUNK = 256
VMEM_LIMIT = 56 * 1024 * 1024

(WB_MQ, WB_MK, WB_MV, WB_MO, WB_CB, WB_CC, WB_CH, WB_AQ, WB_AK, WB_AV, WB_HQ, WB_HI, WB_HG) = range(13)
N_W = 13 * GW
(CB_MQ, CB_MK, CB_MV, CB_MO, CB_AQ, CB_AK, CB_AV, CB_HQ, CB_HI, CB_HG) = range(10)
N_MAIN = 10 * GW
N_AUX = 512


def _rms(x, g):
    return x * lax.rsqrt(jnp.mean(x * x, axis=-1, keepdims=True) + RMS_EPS) * g


def _nt(a, b):
    return lax.dot_general(a, b, (((1,), (1,)), ((), ())), preferred_element_type=F32)


def _tn(a, b):
    return lax.dot_general(a, b, (((0,), (0,)), ((), ())), preferred_element_type=F32)


def _dot(a, b):
    return jnp.dot(a, b, preferred_element_type=F32)


def _head_of(shape, axis):
    return lax.shift_right_logical(lax.broadcasted_iota(jnp.int32, shape, axis), 6)


def _expand_heads(cols, width=GW):
    rows = cols[0].shape[0]
    hid = _head_of((rows, width), 1)
    return jnp.where(hid == 0, cols[0], jnp.where(hid == 1, cols[1], jnp.where(hid == 2, cols[2], cols[3])))


def _params(*sem):
    return pltpu.CompilerParams(dimension_semantics=sem, vmem_limit_bytes=VMEM_LIMIT)


def _inproj_body(x_ref, g_ref, wm_ref, bm_ref, wa_ref, ba_ref, cw_ref, om_ref, oa_ref, oc_ref, halo_ref, *, seq_len):
    i = pl.program_id(0)
    tm = x_ref.shape[0]
    xn = _rms(x_ref[...], g_ref[...]).astype(BF16)

    def proj(wb):
        return _dot(xn, wm_ref[:, wb * GW:(wb + 1) * GW]) + bm_ref[:, wb * GW:(wb + 1) * GW]

    ob = 0
    for wb in range(N_W // GW):
        if wb not in (WB_CB, WB_CC, WB_CH):
            om_ref[:, ob * GW:(ob + 1) * GW] = proj(wb).astype(om_ref.dtype)
            ob += 1
    oa_ref[...] = _dot(xn, wa_ref[...]) + ba_ref[...]

    u = proj(WB_CC) * proj(WB_CH)
    halo = jnp.where((i * tm) % seq_len == 0, 0.0, halo_ref[...])
    ue = jnp.concatenate([halo, u], axis=0)
    w = cw_ref[...]
    y = proj(WB_CB) * (w[0:1] * ue[6:6 + tm] + w[1:2] * ue[7:7 + tm] + w[2:3] * u)
    oc_ref[...] = y.astype(oc_ref.dtype)
    halo_ref[...] = u[tm - 8:tm]


def _inproj(x2, gain, wm, bm, wa, ba, conv_w, seq_len, tm):
    m, d = x2.shape
    const = lambda i: (0, 0)
    rows = lambda width: pl.BlockSpec((tm, width), lambda i: (i, 0))
    return pl.pallas_call(
        functools.partial(_inproj_body, seq_len=seq_len),
        grid=(m // tm,),
        in_specs=[rows(d), pl.BlockSpec((1, d), const),
                  pl.BlockSpec((d, N_W), const), pl.BlockSpec((1, N_W), const),
                  pl.BlockSpec((d, N_AUX), const), pl.BlockSpec((1, N_AUX), const),
                  pl.BlockSpec(conv_w.shape, const)],
        out_specs=[rows(N_MAIN), rows(N_AUX), rows(GW)],
        out_shape=[jax.ShapeDtypeStruct((m, N_MAIN), BF16),
                   jax.ShapeDtypeStruct((m, N_AUX), F32),
                   jax.ShapeDtypeStruct((m, GW), BF16)],
        scratch_shapes=[pltpu.VMEM((8, GW), F32)],
        compiler_params=_params("arbitrary"),
        name="inproj",
    )(x2, gain, wm, bm, wa, ba, conv_w)


def _scan_lanes(x, op, seg):
    pos = lax.broadcasted_iota(jnp.int32, x.shape, 1) & (seg - 1)
    s = 1
    while s < seg:
        x = jnp.where(pos >= s, op(x, pltpu.roll(x, s, 1)), x)
        s *= 2
    return x


def _head_rms_gate(hval, gain, gate, bones):
    sq = hval * hval
    hi = sq.astype(BF16)
    lo = (sq - hi.astype(F32)).astype(BF16)
    ms = (_dot(hi, bones) + _dot(lo, bones)) * (1.0 / HEAD_DIM)
    return hval * lax.rsqrt(ms + RMS_EPS) * gain * gate


def _block_ones():
    r = _head_of((GW, GW), 0)
    c = _head_of((GW, GW), 1)
    return r == c


def _split3(x):
    hi = x.astype(BF16).astype(F32)
    r = x - hi
    mid = r.astype(BF16).astype(F32)
    lo = (r - mid).astype(BF16).astype(F32)
    return hi, mid, lo


def _mlstm_body(q_ref, k_ref, v_ref, o_ref, g_ref, gain_ref, y_ref, ct_ref, n_ref, m_ref, g_ref2, c_ref2, cm_ref2):
    s_len = q_ref.shape[1]
    L = MLSTM_CHUNK
    iota = lambda shape, axis: lax.broadcasted_iota(jnp.int32, shape, axis)
    bd = _block_ones()
    bones = jnp.where(bd, 1.0, 0.0).astype(BF16)
    hid_l = _head_of((L, GW), 1)
    hmask = [hid_l == h for h in range(HEADS)]
    tril = iota((L, L), 0) >= iota((L, L), 1)
    row8 = iota((8, L), 0)
    row16 = iota((16, L), 0)
    hid1 = _head_of((1, GW), 1)
    gain = gain_ref[...]
    ones_bd = jnp.where(_head_of((HEADS * L, GW), 1) == iota((HEADS * L, GW), 0) // L, 1.0, 0.0).astype(BF16)
    csel = iota((128, 3 * GW), 1)
    esel = jnp.where(iota((128, 3 * GW), 0) == 16 + 4 * (csel // GW) + (csel % GW) // HEAD_DIM, 1.0, 0.0).astype(BF16)
    ones8 = jnp.ones((8, L), BF16)

    ct_ref[...] = jnp.zeros_like(ct_ref)
    n_ref[...] = jnp.zeros_like(n_ref)
    m_ref[...] = jnp.zeros_like(m_ref)

    for bb in range(q_ref.shape[0]):
        gt = jnp.concatenate([jnp.transpose(g_ref[bb, c * L:(c + 1) * L, :])[0:8] for c in range(s_len // L)], axis=1)
        g = _scan_lanes(jax.nn.log_sigmoid(pltpu.roll(gt, 4, 0)), jnp.add, L)
        cc = gt - g
        g_ref2[bb] = g
        c_ref2[bb] = cc
        cm_ref2[bb] = _scan_lanes(cc, jnp.maximum, L)

    def one_batch(bb, r0):
        q = q_ref[bb, pl.ds(r0, L), :]
        k = (k_ref[bb, pl.ds(r0, L), :].astype(F32) * (HEAD_DIM ** -0.5)).astype(BF16)
        v = v_ref[bb, pl.ds(r0, L), :]
        g = g_ref2[bb, :, pl.ds(r0, L)]
        cc = c_ref2[bb, :, pl.ds(r0, L)]
        mprev = m_ref[bb][:, 0:1]
        pp = jnp.maximum(mprev, cm_ref2[bb, :, pl.ds(r0, L)])
        ein = jnp.exp(mprev - pp)
        p_last = pp[:, L - 1:L]
        ws = jnp.exp(cc - p_last)
        decay = jnp.exp(mprev - p_last)
        m_new = g[:, L - 1:L] + p_last
        emt = jnp.exp(-(g + pp))
        p_hi, p_mid, p_lo = _split3(pp)
        c_hi, c_mid, c_lo = _split3(cc)

        b0 = jnp.where(row8 < 4, p_hi, pltpu.roll(p_mid, 4, 0))
        b1 = jnp.where(row8 < 4, p_lo, jnp.where(row8 < 7, 1.0, 0.0))
        b2 = jnp.where(row8 < 4, ein, pltpu.roll(ws, 4, 0))
        b3 = jnp.where(row8 < 4, emt, 0.0)
        cols = jnp.transpose(jnp.concatenate([b0, b1, b2, b3, jnp.zeros((96, L), F32)], axis=0)).astype(BF16)

        ws_list = []
        for h in range(HEADS):
            pick = (row16 == h) | (row16 == 4 + h) | (row16 == 8 + h)
            bh = jnp.where(pick, -1.0, jnp.where(row16 == 12, c_hi[h:h + 1], jnp.where(
                row16 == 13, c_mid[h:h + 1], jnp.where(row16 == 14, c_lo[h:h + 1], 0.0))))
            bh = jnp.concatenate([bh, jnp.zeros((112, L), F32)], axis=0).astype(BF16)
            expo = _dot(cols, bh)
            sc = _nt(jnp.where(hmask[h], q, jnp.zeros_like(q)), k)
            ws_list.append((jnp.exp(jnp.where(tril, expo, NEG_BIG)) * sc).astype(BF16))
        w_cat = jnp.concatenate(ws_list, axis=1)
        v_bd = jnp.concatenate([jnp.where(hmask[h], v, jnp.zeros_like(v)) for h in range(HEADS)], axis=0)
        num = _dot(w_cat, v_bd)
        den = _dot(w_cat, ones_bd)
        x3 = _dot(cols, esel)
        ein_x, ws_x, emt_x = x3[:, :GW], x3[:, GW:2 * GW], x3[:, 2 * GW:]

        ct = ct_ref[bb]
        nrow = n_ref[bb][0:1]
        nbd = jnp.where(bd, jnp.broadcast_to(nrow, (GW, GW)), 0.0).astype(BF16)
        num = num + ein_x * _nt(q, ct.astype(BF16))
        den = den + ein_x * _nt(q, nbd)
        hval = num / jnp.maximum(jnp.abs(den), emt_x)
        gate = jax.nn.sigmoid(o_ref[bb, pl.ds(r0, L), :].astype(F32))
        y_ref[bb, pl.ds(r0, L), :] = _head_rms_gate(hval, gain, gate, bones).astype(y_ref.dtype)

        dec_l = jnp.where(hid1 == 0, decay[0:1], jnp.where(hid1 == 1, decay[1:2], jnp.where(hid1 == 2, decay[2:3], decay[3:4])))
        vw = (v.astype(F32) * ws_x).astype(BF16)
        kw = (k.astype(F32) * ws_x).astype(BF16)
        ct_ref[bb] = ct * dec_l + jnp.where(bd, _tn(vw, k), 0.0)
        n_ref[bb] = jnp.broadcast_to(nrow * dec_l + _dot(ones8, kw)[0:1], n_ref.shape[1:])
        m_ref[bb] = jnp.broadcast_to(m_new, m_ref.shape[1:])

    def step(c, carry):
        r0 = pl.multiple_of(c * L, L)
        for bb in range(q_ref.shape[0]):
            one_batch(bb, r0)
        return carry

    lax.fori_loop(0, s_len // L, step, 0)


def _mlstm(main3, aux3, gain):
    b, s, _ = main3.shape
    nb = MLSTM_BATCH if b % MLSTM_BATCH == 0 else 1
    col = lambda cb: pl.BlockSpec((nb, s, GW), lambda i: (i, 0, cb))
    return pl.pallas_call(
        _mlstm_body,
        grid=(b // nb,),
        in_specs=[col(CB_MQ), col(CB_MK), col(CB_MV), col(CB_MO),
                  pl.BlockSpec((nb, s, 128), lambda i: (i, 0, 2)),
                  pl.BlockSpec((1, GW), lambda i: (0, 0))],
        out_specs=pl.BlockSpec((nb, s, GW), lambda i: (i, 0, 0)),
        out_shape=jax.ShapeDtypeStruct((b, s, GW), BF16),
        scratch_shapes=[pltpu.VMEM((nb, GW, GW), F32), pltpu.VMEM((nb, 8, GW), F32), pltpu.VMEM((nb, 8, 128), F32)]
        + [pltpu.VMEM((nb, 8, s), F32)] * 3,
        compiler_params=_params("parallel"),
        name="mlstm",
    )(main3, main3, main3, main3, aux3, gain)


def _hgrn_body(lbl_ref, q_ref, i_ref, g_ref, f_ref, gain_ref, y_ref, st_ref, *, layer):
    nbat, s_len = q_ref.shape[0], q_ref.shape[1]
    C = HGRN_CHUNK
    R = nbat * C
    bd = _block_ones()
    bones = jnp.where(bd, 1.0, 0.0).astype(BF16)
    hid_r = _head_of((R, GW), 1)
    hm_b = [jnp.where(hid_r == h, 1.0, 0.0).astype(BF16) for h in range(HEADS)]
    hid_c = _head_of((C, GW), 1)
    hmask = [hid_c == h for h in range(HEADS)]
    gain = gain_ref[...]

    lg = lbl_ref[...]
    ex = jnp.exp(lg - jnp.max(lg, axis=0, keepdims=True))
    soft = ex / jnp.sum(ex, axis=0, keepdims=True)
    lb = jnp.zeros((1, GW), F32)
    for j in range(1, layer + 1):
        lb = lb + soft[j:j + 1]

    tq = lax.broadcasted_iota(jnp.int32, (HEADS * C, C), 0) & (C - 1)
    tk = lax.broadcasted_iota(jnp.int32, (HEADS * C, C), 1)
    level_mask = {}
    bsz = C
    while bsz >= 2:
        half = bsz // 2
        same = (tq // bsz) == (tk // bsz)
        level_mask[bsz] = same & ((tq % bsz) >= half) & ((tk % bsz) < half)
        bsz = half
    eye = tq == tk
    rowpos = lax.broadcasted_iota(jnp.int32, (R, GW), 0) & (C - 1)

    st_ref[...] = jnp.zeros_like(st_ref)

    def load(ref, r0):
        return ref[:, pl.ds(r0, C), :].reshape(R, GW)

    def stack_heads(a, bb):
        return jnp.concatenate([(a * hm_b[h])[bb * C:(bb + 1) * C] for h in range(HEADS)], axis=0)

    def step(c, carry):
        r0 = pl.multiple_of(c * C, C)
        f = lb + (1.0 - lb) * jax.nn.sigmoid(load(f_ref, r0))
        lf = jnp.log2(f)
        kk = 1.0 - f
        qq = jax.nn.silu(load(q_ref, r0).astype(F32))
        vv = load(i_ref, r0)
        rows = [slice(bb * C, (bb + 1) * C) for bb in range(nbat)]

        gcum = lf
        shift = 1
        while shift < C:
            gcum = gcum + jnp.where(rowpos >= shift, pltpu.roll(gcum, shift, 0), 0.0)
            shift *= 2
        st = [st_ref[bb] for bb in range(nbat)]
        qin = (qq * jnp.exp2(gcum)).astype(BF16)
        out = [_nt(qin[rows[bb]], st[bb].astype(BF16)) for bb in range(nbat)]

        qb, kb = qq.astype(BF16), kk.astype(BF16)
        att = [jnp.where(eye, _nt(stack_heads(qb, bb), kb[rows[bb]]), 0.0) for bb in range(nbat)]
        ge = gcum
        half = 1
        while half < C:
            qd = (qq * jnp.exp2(jnp.minimum(gcum - pltpu.roll(ge, half, 0), 0.0))).astype(BF16)
            kd = (kk * jnp.exp2(ge - gcum)).astype(BF16)
            att = [jnp.where(level_mask[2 * half], _nt(stack_heads(qd, bb), kd[rows[bb]]), att[bb]) for bb in range(nbat)]
            ge = jnp.where((rowpos & half) != 0, ge, pltpu.roll(ge, R - half, 0))
            half *= 2
        for bb in range(nbat):
            ab = att[bb].astype(BF16)
            for h in range(HEADS):
                out[bb] = out[bb] + jnp.where(hmask[h], _dot(ab[h * C:(h + 1) * C], vv[rows[bb]]), 0.0)

        gate = jax.nn.silu(load(g_ref, r0).astype(F32))
        y = _head_rms_gate(jnp.concatenate(out, axis=0), gain, gate, bones).astype(y_ref.dtype)
        y_ref[:, pl.ds(r0, C), :] = y.reshape(nbat, C, GW)

        for bb in range(nbat):
            g_last = gcum[bb * C + C - 1:bb * C + C, :]
            kdec = (kk[rows[bb]] * jnp.exp2(g_last - gcum[rows[bb]])).astype(BF16)
            st_ref[bb] = st[bb] * jnp.exp2(g_last) + jnp.where(bd, _tn(vv[rows[bb]], kdec), 0.0)
        return carry

    lax.fori_loop(0, s_len // C, step, 0)


def _hgrn(main3, aux3, lb_logits, gain, layer):
    b, s, _ = main3.shape
    nbat = HGRN_BATCH if b % HGRN_BATCH == 0 else 1
    col = lambda cb: pl.BlockSpec((nbat, s, GW), lambda i: (i, 0, cb))
    return pl.pallas_call(
        functools.partial(_hgrn_body, layer=layer),
        grid=(b // nbat,),
        in_specs=[pl.BlockSpec(lb_logits.shape, lambda i: (0, 0)),
                  col(CB_HQ), col(CB_HI), col(CB_HG),
                  pl.BlockSpec((nbat, s, GW), lambda i: (i, 0, 0)),
                  pl.BlockSpec((1, GW), lambda i: (0, 0))],
        out_specs=pl.BlockSpec((nbat, s, GW), lambda i: (i, 0, 0)),
        out_shape=jax.ShapeDtypeStruct((b, s, GW), BF16),
        scratch_shapes=[pltpu.VMEM((nbat, GW, GW), F32)],
        compiler_params=_params("parallel"),
        name="hgrn2",
    )(lb_logits, main3, main3, main3, aux3, gain)


def _bucket_tables():
    def bucket(dist):
        n = np.maximum(dist, 0)
        exact = REL_BUCKETS // 2
        nf = np.maximum(n, 1).astype(np.float32)
        large = exact + (np.log(nf / np.float32(exact)) / np.float32(math.log(REL_MAX_DIST / exact))
                         * np.float32(REL_BUCKETS - exact)).astype(np.int32)
        large = np.minimum(large, REL_BUCKETS - 1)
        return np.where(n < exact, n, large).astype(np.int32)
    tk = np.arange(MOBA_BLOCK)[:, None]
    tq = np.arange(MOBA_BLOCK)[None, :]
    own = np.where(tq - tk >= 0, bucket(tq - tk), REL_BUCKETS)
    adj = bucket(MOBA_BLOCK + tq - tk)
    far = int(bucket(np.array([2 * MOBA_BLOCK]))[0])
    return np.stack([own, adj]).astype(np.int32), far


def _bias_body(rb_ref, bk_ref, o_ref):
    bk = bk_ref[...]
    bs = bk.shape[-1]
    for h in range(HEADS):
        acc = jnp.full(bk.shape, NEG_BIG, F32)
        for j in range(REL_BUCKETS):
            acc = jnp.where(bk == j, rb_ref[j, h] * LOG2E, acc)
        o_ref[:, :, h * bs:(h + 1) * bs] = acc


def _bias_tiles(rel_bias, buckets):
    two, bk, bq = buckets.shape
    return pl.pallas_call(
        _bias_body,
        in_specs=[pl.BlockSpec(memory_space=pltpu.SMEM), pl.BlockSpec(buckets.shape, lambda: (0, 0, 0))],
        out_specs=pl.BlockSpec((two, bk, HEADS * bq), lambda: (0, 0, 0)),
        out_shape=jax.ShapeDtypeStruct((two, bk, HEADS * bq), F32),
        name="moba_bias",
    )(rel_bias, jnp.asarray(buckets))


VT_ROWS = HEAD_DIM + 16


def _moba_body(rb_ref, q_ref, k_ref, v_ref, bias_ref, y_ref, vt_ref, km_ref, qbd_ref, *, far_bucket):
    i = pl.program_id(1)
    BS = MOBA_BLOCK
    nb = k_ref.shape[1] // BS
    n_sel = min(MOBA_TOPK, nb - 1)
    scale = HEAD_DIM ** -0.5

    @pl.when(i == 0)
    def _():
        hid = _head_of((1, GW), 1)
        vt_ref[:, HEAD_DIM:, :] = jnp.ones((HEADS, VT_ROWS - HEAD_DIM, nb * BS), BF16)
        for n in range(nb):
            cols = slice(n * BS, (n + 1) * BS)
            vt = jnp.transpose(v_ref[0, cols, :].astype(F32)).astype(BF16)
            km = jnp.mean(k_ref[0, cols, :].astype(F32), axis=0, keepdims=True)
            for h in range(HEADS):
                vt_ref[h, 0:HEAD_DIM, cols] = vt[h * HEAD_DIM:(h + 1) * HEAD_DIM]
                km_ref[h * nb + n:h * nb + n + 1, :] = jnp.where(hid == h, km, 0.0).astype(BF16)

    q = q_ref[0]
    hid_q = _head_of((BS, GW), 1)
    qs = (q.astype(F32) * (scale * LOG2E)).astype(BF16)
    for h in range(HEADS):
        qbd_ref[h * BS:(h + 1) * BS, :] = jnp.where(hid_q == h, qs, jnp.zeros_like(qs))

    gate = _nt(km_ref[...], q)
    blk = lax.broadcasted_iota(jnp.int32, (nb, BS), 0)
    sel = []
    for h in range(HEADS):
        g = jnp.where(blk < i, gate[h * nb:(h + 1) * nb], NEG_BIG)
        rank = jnp.zeros((nb, BS), jnp.int32)
        for mrow in range(nb):
            gm = g[mrow:mrow + 1]
            ahead = (gm > g) | ((gm == g) & (mrow < blk))
            rank = rank + jnp.where(ahead, 1, 0)
        chosen = (rank < n_sel) & (blk < i)
        far = jnp.where(blk < i - 1, rb_ref[far_bucket, h] * LOG2E, 0.0)
        sel.append(jnp.where(chosen, far, NEG_BIG))
    sel = jnp.concatenate(sel, axis=1)

    def attend(n_past):
        nblk = n_past + 1
        keys = k_ref[0, 0:nblk * BS, :]
        s_all = [_nt(keys, qbd_ref[h * BS:(h + 1) * BS, :]) for h in range(HEADS)]
        m = [None] * HEADS
        acc = [None] * HEADS
        for n in range(nblk):
            p, alpha = [None] * HEADS, [None] * HEADS
            for h in range(HEADS):
                hs = slice(h * BS, (h + 1) * BS)
                s = s_all[h][n * BS:(n + 1) * BS]
                if n == n_past:
                    s, r = s + bias_ref[0, :, hs], None
                elif n == n_past - 1:
                    s, r = s + bias_ref[1, :, hs], sel[n:n + 1, hs]
                else:
                    r = sel[n:n + 1, hs]
                bm = jnp.max(s, axis=0, keepdims=True)
                bm = bm if r is None else bm + r
                m_new = bm if m[h] is None else jnp.maximum(m[h], bm)
                p[h] = jnp.exp2(s - (m_new if r is None else m_new - r)).astype(BF16)
                alpha[h] = None if m[h] is None else jnp.exp2(m[h] - m_new)
                m[h] = m_new
            for h in range(HEADS):
                pv = _dot(vt_ref[h, :, n * BS:(n + 1) * BS], p[h])
                acc[h] = pv if alpha[h] is None else alpha[h] * acc[h] + pv
        outs = [a[0:HEAD_DIM] * (1.0 / a[HEAD_DIM:HEAD_DIM + 1]) for a in acc]
        y_ref[0] = jnp.transpose(jnp.concatenate(outs, axis=0)).astype(y_ref.dtype)

    for n_past in range(nb):
        pl.when(i == n_past)(functools.partial(attend, n_past))


def _moba(main3, rel_bias, bias_tiles, far_bucket):
    b, s, _ = main3.shape
    nb = s // MOBA_BLOCK
    return pl.pallas_call(
        functools.partial(_moba_body, far_bucket=far_bucket),
        grid=(b, nb),
        in_specs=[pl.BlockSpec(memory_space=pltpu.SMEM),
                  pl.BlockSpec((1, MOBA_BLOCK, GW), lambda bi, i: (bi, i, CB_AQ)),
                  pl.BlockSpec((1, s, GW), lambda bi, i: (bi, 0, CB_AK)),
                  pl.BlockSpec((1, s, GW), lambda bi, i: (bi, 0, CB_AV)),
                  pl.BlockSpec(bias_tiles.shape, lambda bi, i: (0, 0, 0), pipeline_mode=pl.Buffered(1))],
        out_specs=pl.BlockSpec((1, MOBA_BLOCK, GW), lambda bi, i: (bi, i, 0)),
        out_shape=jax.ShapeDtypeStruct((b, s, GW), BF16),
        scratch_shapes=[pltpu.VMEM((HEADS, VT_ROWS, s), BF16),
                        pltpu.VMEM((HEADS * nb, GW), BF16),
                        pltpu.VMEM((HEADS * MOBA_BLOCK, GW), BF16)],
        compiler_params=_params("parallel", "arbitrary"),
        name="moba",
    )(rel_bias, main3, main3, main3, bias_tiles)


def _memkv_body(mem_ref, g_ref, wk_ref, wv_ref, k_ref, v_ref):
    mn = _rms(mem_ref[0], g_ref[...]).astype(BF16)
    k_ref[0] = _dot(mn, wk_ref[...]).astype(BF16)
    v_ref[0] = _dot(mn, wv_ref[...]).astype(BF16)


def _memkv(mem, gain, wk, wv):
    b, nm, d = mem.shape
    cw = wk.shape[1]
    const = lambda i: (0, 0)
    return pl.pallas_call(
        _memkv_body,
        grid=(b,),
        in_specs=[pl.BlockSpec((1, nm, d), lambda i: (i, 0, 0)), pl.BlockSpec((1, d), const),
                  pl.BlockSpec(wk.shape, const), pl.BlockSpec(wv.shape, const)],
        out_specs=[pl.BlockSpec((1, nm, cw), lambda i: (i, 0, 0))] * 2,
        out_shape=[jax.ShapeDtypeStruct((b, nm, cw), BF16)] * 2,
        compiler_params=_params("parallel"),
        name="memkv",
    )(mem, gain, wk, wv)


def _mixout_cross_body(ym_ref, yc_ref, ya_ref, yh_ref, wout_ref, gmix_ref, x_ref,
                       gpre_ref, wq_ref, k_ref, v_ref, wo_ref, gpost_ref, o_ref):
    h = (_dot(ym_ref[0], wout_ref[0 * GW:1 * GW, :]) + _dot(yc_ref[0], wout_ref[1 * GW:2 * GW, :])
         + _dot(ya_ref[0], wout_ref[2 * GW:3 * GW, :]) + _dot(yh_ref[0], wout_ref[3 * GW:4 * GW, :]))
    x = x_ref[0] + _rms(h, gmix_ref[...])
    xn = _rms(x, gpre_ref[...]).astype(BF16)
    q = _dot(xn, wq_ref[...]).astype(BF16)
    k = k_ref[0]
    v = v_ref[0]
    outs = []
    for hh in range(CROSS_HEADS):
        sl = slice(hh * CROSS_HEAD_DIM, (hh + 1) * CROSS_HEAD_DIM)
        s = _nt(q[:, sl], k[:, sl]) * (CROSS_HEAD_DIM ** -0.5)
        p = jnp.exp(s - jnp.max(s, axis=-1, keepdims=True))
        inv = 1.0 / jnp.sum(p, axis=-1, keepdims=True)
        outs.append((_dot(p.astype(BF16), v[:, sl]) * inv).astype(BF16))
    o = jnp.concatenate(outs, axis=-1)
    o_ref[0] = x + _rms(_dot(o, wo_ref[...]), gpost_ref[...])


def _mixout_cross(ys, wout, gmix, x3, gpre, wq, km, vm, wo, gpost, tq):
    b, s, d = x3.shape
    nm, cw = km.shape[1:]
    const = lambda bi, i: (0, 0)
    tile = lambda width: pl.BlockSpec((1, tq, width), lambda bi, i: (bi, i, 0))
    mem = pl.BlockSpec((1, nm, cw), lambda bi, i: (bi, 0, 0))
    return pl.pallas_call(
        _mixout_cross_body,
        grid=(b, s // tq),
        in_specs=[tile(GW), tile(GW), tile(GW), tile(GW),
                  pl.BlockSpec(wout.shape, const), pl.BlockSpec((1, d), const), tile(d),
                  pl.BlockSpec((1, d), const), pl.BlockSpec(wq.shape, const), mem, mem,
                  pl.BlockSpec(wo.shape, const), pl.BlockSpec((1, d), const)],
        out_specs=tile(d),
        out_shape=jax.ShapeDtypeStruct((b, s, d), F32),
        compiler_params=_params("parallel", "parallel"),
        name="mixout_cross",
    )(*ys, wout, gmix, x3, gpre, wq, km, vm, wo, gpost)


def _ffn_body(x_ref, xh_ref, gpre_ref, wg_ref, wu_ref, cw_ref, cb_ref, wd_ref, gpost_ref, o_ref,
              xn_ref, act_ref, *, seq_len):
    i = pl.program_id(0)
    tm = x_ref.shape[0]
    g = gpre_ref[...]
    x = x_ref[...]
    xn_ref[8:, :] = _rms(x, g).astype(BF16)
    starts_seq = (i * tm) % seq_len == 0
    xn_ref[0:8, :] = jnp.where(starts_seq, 0.0, _rms(xh_ref[...], g)).astype(BF16)
    xe = xn_ref[...]
    for c in range(0, act_ref.shape[1], FFN_CHUNK):
        cs = slice(c, c + FFN_CHUNK)
        ge = _dot(xe, wg_ref[:, cs])
        up = _dot(xe[8:], wu_ref[:, cs])
        cw = cw_ref[:, cs]
        gate = cw[0:1] * ge[6:6 + tm] + cw[1:2] * ge[7:7 + tm] + cw[2:3] * ge[8:8 + tm] + cb_ref[:, cs]
        act_ref[:, cs] = (jax.nn.silu(gate) * up).astype(BF16)
    o_ref[...] = x + _rms(_dot(act_ref[...], wd_ref[...]), gpost_ref[...])


def _ffn(x2, gpre, wg, wu, cw, cb, wd, gpost, seq_len, tm):
    m, d = x2.shape
    f = wg.shape[1]
    const = lambda i: (0, 0)
    resident = lambda shape: pl.BlockSpec(shape, const, pipeline_mode=pl.Buffered(1))
    return pl.pallas_call(
        functools.partial(_ffn_body, seq_len=seq_len),
        grid=(m // tm,),
        in_specs=[pl.BlockSpec((tm, d), lambda i: (i, 0)),
                  pl.BlockSpec((8, d), lambda i: (jnp.maximum(i * (tm // 8) - 1, 0), 0)),
                  pl.BlockSpec((1, d), const),
                  resident((d, f)), resident((d, f)),
                  pl.BlockSpec((3, f), const), pl.BlockSpec((1, f), const),
                  resident((f, d)),
                  pl.BlockSpec((1, d), const)],
        out_specs=pl.BlockSpec((tm, d), lambda i: (i, 0)),
        out_shape=jax.ShapeDtypeStruct((m, d), F32),
        scratch_shapes=[pltpu.VMEM((tm + 8, d), BF16), pltpu.VMEM((tm, f), BF16)],
        compiler_params=_params("parallel"),
        name="ffn",
    )(x2, x2, gpre, wg, wu, cw, cb, wd, gpost)


def _row_tile(m, pref):
    t = pref
    while m % t:
        t //= 2
    return t


def kernel(x, mem, w_in, b_in, mlstm_norm, sconv_w, rel_bias, hgrn_lb_logits, hgrn_norm, w_mix_out, norm_mix_pre, norm_mix_post, mem_norm, w_cq, w_ck, w_cv, w_co, norm_cross_pre, norm_cross_post, w_ffn_in, ffn_conv_w, ffn_conv_b, w_ffn_out, norm_ffn_pre, norm_ffn_post):
    b, s, d = x.shape
    depth = w_in.shape[0]
    d_ff = w_ffn_out.shape[1]
    assert s % MOBA_BLOCK == 0 and s % MLSTM_CHUNK == 0 and s % HGRN_CHUNK == 0
    m = b * s
    tm = _row_tile(m, 512)

    gate0 = 4 * GW
    hq0 = gate0 + 2 * HEADS + 6 * GW
    main_cols = lambda a: jnp.concatenate([a[..., :gate0], a[..., gate0 + 2 * HEADS:hq0 + GW], a[..., hq0 + 2 * GW:]], axis=-1)
    aux_cols = lambda a: jnp.concatenate(
        [a[..., hq0 + GW:hq0 + 2 * GW], a[..., gate0:gate0 + 2 * HEADS],
         jnp.zeros(a.shape[:-1] + (N_AUX - GW - 2 * HEADS,), a.dtype)], axis=-1)
    wm_all = main_cols(w_in).astype(BF16)
    wa_all = aux_cols(w_in).astype(BF16)
    bm_all = main_cols(b_in)[:, None, :]
    ba_all = aux_cols(b_in)[:, None, :]

    w_out = w_mix_out.astype(BF16)
    wq, wk, wv, wo = (w.astype(BF16) for w in (w_cq, w_ck, w_cv, w_co))
    w_gate = w_ffn_in[..., :d_ff].astype(BF16)
    w_up = w_ffn_in[..., d_ff:].astype(BF16)
    w_down = w_ffn_out.astype(BF16)
    row = lambda a, l: a[l][None, :]

    buckets, far_bucket = _bucket_tables()
    bias_tiles = _bias_tiles(rel_bias, buckets)
    assert d_ff % FFN_CHUNK == 0

    x2 = x.reshape(m, d)
    for l in range(depth):
        main, aux, y_c = _inproj(x2, row(norm_mix_pre, l), wm_all[l], bm_all[l], wa_all[l], ba_all[l], sconv_w[l], s, tm)
        y_c = y_c.reshape(b, s, GW)
        main3 = main.reshape(b, s, N_MAIN)
        aux3 = aux.reshape(b, s, N_AUX)
        y_m = _mlstm(main3, aux3, row(mlstm_norm, l))
        y_a = _moba(main3, rel_bias, bias_tiles, far_bucket)
        y_h = _hgrn(main3, aux3, hgrn_lb_logits, row(hgrn_norm, l), l)
        km, vm = _memkv(mem, row(mem_norm, l), wk[l], wv[l])
        x2 = _mixout_cross((y_m, y_c, y_a, y_h), w_out[l], row(norm_mix_post, l), x2.reshape(b, s, d),
                           row(norm_cross_pre, l), wq[l], km, vm, wo[l], row(norm_cross_post, l),
                           _row_tile(s, 512)).reshape(m, d)

        x2 = _ffn(x2, row(norm_ffn_pre, l), w_gate[l], w_up[l], ffn_conv_w[l], row(ffn_conv_b, l),
                  w_down[l], row(norm_ffn_post, l), s, tm)
    return x2.reshape(b, s, d)
```

```python
import functools
import math

import numpy as np
import jax
import jax.numpy as jnp
from jax import lax
from jax.experimental import pallas as pl
from jax.experimental.pallas import tpu as pltpu

F32 = jnp.float32
BF16 = jnp.bfloat16

HEADS = 4
HEAD_DIM = 64
GW = HEADS * HEAD_DIM
MOBA_BLOCK = 256
MOBA_TOPK = 3
MOBA_TILES = 2
REL_BUCKETS = 32
REL_MAX_DIST = 128
CROSS_HEADS = 4
CROSS_HEAD_DIM = 128
RMS_EPS = 1e-6
NEG_BIG = -1e30
LOG2E = 1.4426950408889634

MLSTM_CHUNK = 256
MLSTM_BATCH = 2
HGRN_CHUNK = 64
HGRN_BATCH = 2
FFN_CHUNK = 256
FFN_ROWS = 512
VMEM_LIMIT = 56 * 1024 * 1024

(WB_MQ, WB_MK, WB_MV, WB_MO, WB_CB, WB_CC, WB_CH, WB_AQ, WB_AK, WB_AV, WB_HQ, WB_HI, WB_HG) = range(13)
N_W = 13 * GW
(CB_MQ, CB_MK, CB_MV, CB_MO, CB_AQ, CB_AK, CB_AV, CB_HQ, CB_HI, CB_HG) = range(10)
N_MAIN = 10 * GW
N_AUX = 512


def _rms(x, g):
    return x * lax.rsqrt(jnp.mean(x * x, axis=-1, keepdims=True) + RMS_EPS) * g


def _nt(a, b):
    return lax.dot_general(a, b, (((1,), (1,)), ((), ())), preferred_element_type=F32)


def _tn(a, b):
    return lax.dot_general(a, b, (((0,), (0,)), ((), ())), preferred_element_type=F32)


def _dot(a, b):
    return jnp.dot(a, b, preferred_element_type=F32)


def _head_of(shape, axis):
    return lax.shift_right_logical(lax.broadcasted_iota(jnp.int32, shape, axis), 6)


def _params(*sem):
    return pltpu.CompilerParams(dimension_semantics=sem, vmem_limit_bytes=VMEM_LIMIT)


def _inproj_body(x_ref, g_ref, wm_ref, bm_ref, wa_ref, ba_ref, cw_ref, om_ref, oa_ref, oc_ref, halo_ref, *, seq_len):
    i = pl.program_id(0)
    tm = x_ref.shape[0]
    xn = _rms(x_ref[...], g_ref[...]).astype(BF16)

    def proj(wb):
        return _dot(xn, wm_ref[:, wb * GW:(wb + 1) * GW]) + bm_ref[:, wb * GW:(wb + 1) * GW]

    ob = 0
    for wb in range(N_W // GW):
        if wb not in (WB_CB, WB_CC, WB_CH):
            om_ref[:, ob * GW:(ob + 1) * GW] = proj(wb).astype(om_ref.dtype)
            ob += 1
    oa_ref[...] = _dot(xn, wa_ref[...]) + ba_ref[...]

    @pl.when(i == 0)
    def _():
        halo_ref[...] = jnp.zeros_like(halo_ref)

    u = proj(WB_CC) * proj(WB_CH)
    halo = jnp.where((i * tm) % seq_len == 0, 0.0, halo_ref[...])
    ue = jnp.concatenate([halo, u], axis=0)
    w = cw_ref[...]
    y = proj(WB_CB) * (w[0:1] * ue[6:6 + tm] + w[1:2] * ue[7:7 + tm] + w[2:3] * u)
    oc_ref[...] = y.astype(oc_ref.dtype)
    halo_ref[...] = u[tm - 8:tm]


def _inproj(x2, gain, wm, bm, wa, ba, conv_w, seq_len, tm):
    m, d = x2.shape
    const = lambda i: (0, 0)
    rows = lambda width: pl.BlockSpec((tm, width), lambda i: (i, 0))
    return pl.pallas_call(
        functools.partial(_inproj_body, seq_len=seq_len),
        grid=(m // tm,),
        in_specs=[rows(d), pl.BlockSpec((1, d), const),
                  pl.BlockSpec((d, N_W), const), pl.BlockSpec((1, N_W), const),
                  pl.BlockSpec((d, N_AUX), const), pl.BlockSpec((1, N_AUX), const),
                  pl.BlockSpec(conv_w.shape, const)],
        out_specs=[rows(N_MAIN), rows(N_AUX), rows(GW)],
        out_shape=[jax.ShapeDtypeStruct((m, N_MAIN), BF16),
                   jax.ShapeDtypeStruct((m, N_AUX), F32),
                   jax.ShapeDtypeStruct((m, GW), BF16)],
        scratch_shapes=[pltpu.VMEM((8, GW), F32)],
        compiler_params=_params("arbitrary"),
        name="inproj",
    )(x2, gain, wm, bm, wa, ba, conv_w)


def _scan_lanes(x, op, seg):
    pos = lax.broadcasted_iota(jnp.int32, x.shape, 1) & (seg - 1)
    s = 1
    while s < seg:
        x = jnp.where(pos >= s, op(x, pltpu.roll(x, s, 1)), x)
        s *= 2
    return x


def _head_rms_gate(hval, gain, gate, bones):
    sq = hval * hval
    hi = sq.astype(BF16)
    lo = (sq - hi.astype(F32)).astype(BF16)
    ms = (_dot(hi, bones) + _dot(lo, bones)) * (1.0 / HEAD_DIM)
    return hval * lax.rsqrt(ms + RMS_EPS) * gain * gate


def _block_ones():
    r = _head_of((GW, GW), 0)
    c = _head_of((GW, GW), 1)
    return r == c


def _split3(x):
    hi = x.astype(BF16).astype(F32)
    r = x - hi
    mid = r.astype(BF16).astype(F32)
    lo = (r - mid).astype(BF16).astype(F32)
    return hi, mid, lo


def _mlstm_body(q_ref, k_ref, v_ref, o_ref, g_ref, gain_ref, y_ref, ct_ref, n_ref, m_ref, g_ref2, c_ref2, cm_ref2):
    s_len = q_ref.shape[1]
    L = MLSTM_CHUNK
    iota = lambda shape, axis: lax.broadcasted_iota(jnp.int32, shape, axis)
    bd = _block_ones()
    bones = jnp.where(bd, 1.0, 0.0).astype(BF16)
    hid_l = _head_of((L, GW), 1)
    hmask = [hid_l == h for h in range(HEADS)]
    tril = iota((L, L), 0) >= iota((L, L), 1)
    row8 = iota((8, L), 0)
    row16 = iota((16, L), 0)
    hid1 = _head_of((1, GW), 1)
    gain = gain_ref[...]
    ones_bd = jnp.where(_head_of((HEADS * L, GW), 1) == iota((HEADS * L, GW), 0) // L, 1.0, 0.0).astype(BF16)
    csel = iota((128, 3 * GW), 1)
    esel = jnp.where(iota((128, 3 * GW), 0) == 16 + 4 * (csel // GW) + (csel % GW) // HEAD_DIM, 1.0, 0.0).astype(BF16)
    ones8 = jnp.ones((8, L), BF16)

    ct_ref[...] = jnp.zeros_like(ct_ref)
    n_ref[...] = jnp.zeros_like(n_ref)
    m_ref[...] = jnp.zeros_like(m_ref)

    for bb in range(q_ref.shape[0]):
        gt = jnp.concatenate([jnp.transpose(g_ref[bb, c * L:(c + 1) * L, :])[0:8] for c in range(s_len // L)], axis=1)
        g = _scan_lanes(jax.nn.log_sigmoid(pltpu.roll(gt, 4, 0)), jnp.add, L)
        cc = gt - g
        g_ref2[bb] = g
        c_ref2[bb] = cc
        cm_ref2[bb] = _scan_lanes(cc, jnp.maximum, L)

    def prep(bb, r0):
        q = q_ref[bb, pl.ds(r0, L), :]
        k = (k_ref[bb, pl.ds(r0, L), :].astype(F32) * (HEAD_DIM ** -0.5)).astype(BF16)
        v = v_ref[bb, pl.ds(r0, L), :]
        g = g_ref2[bb, :, pl.ds(r0, L)]
        cc = c_ref2[bb, :, pl.ds(r0, L)]
        mprev = m_ref[bb][:, 0:1]
        pp = jnp.maximum(mprev, cm_ref2[bb, :, pl.ds(r0, L)])
        ein = jnp.exp(mprev - pp)
        p_last = pp[:, L - 1:L]
        ws = jnp.exp(cc - p_last)
        decay = jnp.exp(mprev - p_last)
        m_new = g[:, L - 1:L] + p_last
        emt = jnp.exp(-(g + pp))
        p_hi, p_mid, p_lo = _split3(pp)
        b0 = jnp.where(row8 < 4, p_hi, pltpu.roll(p_mid, 4, 0))
        b1 = jnp.where(row8 < 4, p_lo, jnp.where(row8 < 7, 1.0, 0.0))
        b2 = jnp.where(row8 < 4, ein, pltpu.roll(ws, 4, 0))
        b3 = jnp.where(row8 < 4, emt, 0.0)
        cols = jnp.transpose(jnp.concatenate([b0, b1, b2, b3, jnp.zeros((96, L), F32)], axis=0)).astype(BF16)
        return dict(bb=bb, r0=r0, q=q, k=k, v=v, cpieces=_split3(cc), cols=cols, decay=decay, m_new=m_new, w=[])

    def head_weights(st, h):
        c_hi, c_mid, c_lo = st["cpieces"]
        pick = (row16 == h) | (row16 == 4 + h) | (row16 == 8 + h)
        bh = jnp.where(pick, -1.0, jnp.where(row16 == 12, c_hi[h:h + 1], jnp.where(
            row16 == 13, c_mid[h:h + 1], jnp.where(row16 == 14, c_lo[h:h + 1], 0.0))))
        bh = jnp.concatenate([bh, jnp.zeros((112, L), F32)], axis=0).astype(BF16)
        expo = _dot(st["cols"], bh)
        sc = _nt(jnp.where(hmask[h], st["q"], jnp.zeros_like(st["q"])), st["k"])
        st["w"].append((jnp.exp(jnp.where(tril, expo, NEG_BIG)) * sc).astype(BF16))

    def finish(st):
        bb, r0, q, k, v, cols = st["bb"], st["r0"], st["q"], st["k"], st["v"], st["cols"]
        w_cat = jnp.concatenate(st["w"], axis=1)
        v_bd = jnp.concatenate([jnp.where(hmask[h], v, jnp.zeros_like(v)) for h in range(HEADS)], axis=0)
        num = _dot(w_cat, v_bd)
        den = _dot(w_cat, ones_bd)
        x3 = _dot(cols, esel)
        ein_x, ws_x, emt_x = x3[:, :GW], x3[:, GW:2 * GW], x3[:, 2 * GW:]

        ct = ct_ref[bb]
        nrow = n_ref[bb][0:1]
        nbd = jnp.where(bd, jnp.broadcast_to(nrow, (GW, GW)), 0.0).astype(BF16)
        num = num + ein_x * _nt(q, ct.astype(BF16))
        den = den + ein_x * _nt(q, nbd)
        hval = num / jnp.maximum(jnp.abs(den), emt_x)
        gate = jax.nn.sigmoid(o_ref[bb, pl.ds(r0, L), :].astype(F32))
        y_ref[bb, pl.ds(r0, L), :] = _head_rms_gate(hval, gain, gate, bones).astype(y_ref.dtype)

        decay = st["decay"]
        dec_l = jnp.where(hid1 == 0, decay[0:1], jnp.where(hid1 == 1, decay[1:2], jnp.where(hid1 == 2, decay[2:3], decay[3:4])))
        vw = (v.astype(F32) * ws_x).astype(BF16)
        kw = (k.astype(F32) * ws_x).astype(BF16)
        ct_ref[bb] = ct * dec_l + jnp.where(bd, _tn(vw, k), 0.0)
        n_ref[bb] = jnp.broadcast_to(nrow * dec_l + _dot(ones8, kw)[0:1], n_ref.shape[1:])
        m_ref[bb] = jnp.broadcast_to(st["m_new"], m_ref.shape[1:])

    def step(c, carry):
        r0 = pl.multiple_of(c * L, L)
        states = [prep(bb, r0) for bb in range(q_ref.shape[0])]
        for h in range(HEADS):
            for st in states:
                head_weights(st, h)
        for st in states:
            finish(st)
        return carry

    lax.fori_loop(0, s_len // L, step, 0)


def _mlstm(main3, aux3, gain):
    b, s, _ = main3.shape
    nb = MLSTM_BATCH if b % MLSTM_BATCH == 0 else 1
    col = lambda cb: pl.BlockSpec((nb, s, GW), lambda i: (i, 0, cb))
    return pl.pallas_call(
        _mlstm_body,
        grid=(b // nb,),
        in_specs=[col(CB_MQ), col(CB_MK), col(CB_MV), col(CB_MO),
                  pl.BlockSpec((nb, s, 128), lambda i: (i, 0, 2)),
                  pl.BlockSpec((1, GW), lambda i: (0, 0))],
        out_specs=pl.BlockSpec((nb, s, GW), lambda i: (i, 0, 0)),
        out_shape=jax.ShapeDtypeStruct((b, s, GW), BF16),
        scratch_shapes=[pltpu.VMEM((nb, GW, GW), F32), pltpu.VMEM((nb, 8, GW), F32), pltpu.VMEM((nb, 8, 128), F32)]
        + [pltpu.VMEM((nb, 8, s), F32)] * 3,
        compiler_params=_params("parallel"),
        name="mlstm",
    )(main3, main3, main3, main3, aux3, gain)


def _hgrn_body(lbl_ref, q_ref, i_ref, g_ref, f_ref, gain_ref, y_ref, st_ref, *, layer):
    nbat, s_len = q_ref.shape[0], q_ref.shape[1]
    C = HGRN_CHUNK
    R = nbat * C
    bd = _block_ones()
    bones = jnp.where(bd, 1.0, 0.0).astype(BF16)
    hid_r = _head_of((R, GW), 1)
    hm_b = [jnp.where(hid_r == h, 1.0, 0.0).astype(BF16) for h in range(HEADS)]
    hid_c = _head_of((C, GW), 1)
    hmask = [hid_c == h for h in range(HEADS)]
    gain = gain_ref[...]

    lg = lbl_ref[...]
    ex = jnp.exp(lg - jnp.max(lg, axis=0, keepdims=True))
    soft = ex / jnp.sum(ex, axis=0, keepdims=True)
    lb = jnp.zeros((1, GW), F32)
    for j in range(1, layer + 1):
        lb = lb + soft[j:j + 1]

    tq = lax.broadcasted_iota(jnp.int32, (HEADS * C, C), 0) & (C - 1)
    tk = lax.broadcasted_iota(jnp.int32, (HEADS * C, C), 1)
    level_mask = {}
    bsz = C
    while bsz >= 2:
        half = bsz // 2
        same = (tq // bsz) == (tk // bsz)
        level_mask[bsz] = same & ((tq % bsz) >= half) & ((tk % bsz) < half)
        bsz = half
    eye = tq == tk
    rowpos = lax.broadcasted_iota(jnp.int32, (R, GW), 0) & (C - 1)

    st_ref[...] = jnp.zeros_like(st_ref)

    def load(ref, r0):
        return ref[:, pl.ds(r0, C), :].reshape(R, GW)

    def stack_heads(a, bb):
        return jnp.concatenate([(a * hm_b[h])[bb * C:(bb + 1) * C] for h in range(HEADS)], axis=0)

    def step(c, carry):
        r0 = pl.multiple_of(c * C, C)
        f = lb + (1.0 - lb) * jax.nn.sigmoid(load(f_ref, r0))
        lf = jnp.log2(f)
        kk = 1.0 - f
        qq = jax.nn.silu(load(q_ref, r0).astype(F32))
        vv = load(i_ref, r0)
        rows = [slice(bb * C, (bb + 1) * C) for bb in range(nbat)]

        gcum = lf
        shift = 1
        while shift < C:
            gcum = gcum + jnp.where(rowpos >= shift, pltpu.roll(gcum, shift, 0), 0.0)
            shift *= 2
        st = [st_ref[bb] for bb in range(nbat)]
        qin = (qq * jnp.exp2(gcum)).astype(BF16)
        out = [_nt(qin[rows[bb]], st[bb].astype(BF16)) for bb in range(nbat)]

        qb, kb = qq.astype(BF16), kk.astype(BF16)
        att = [jnp.where(eye, _nt(stack_heads(qb, bb), kb[rows[bb]]), 0.0) for bb in range(nbat)]
        ge = gcum
        half = 1
        while half < C:
            qd = (qq * jnp.exp2(jnp.minimum(gcum - pltpu.roll(ge, half, 0), 0.0))).astype(BF16)
            kd = (kk * jnp.exp2(ge - gcum)).astype(BF16)
            att = [jnp.where(level_mask[2 * half], _nt(stack_heads(qd, bb), kd[rows[bb]]), att[bb]) for bb in range(nbat)]
            ge = jnp.where((rowpos & half) != 0, ge, pltpu.roll(ge, R - half, 0))
            half *= 2
        for bb in range(nbat):
            ab = att[bb].astype(BF16)
            for h in range(HEADS):
                out[bb] = out[bb] + jnp.where(hmask[h], _dot(ab[h * C:(h + 1) * C], vv[rows[bb]]), 0.0)

        gate = jax.nn.silu(load(g_ref, r0).astype(F32))
        y = _head_rms_gate(jnp.concatenate(out, axis=0), gain, gate, bones).astype(y_ref.dtype)
        y_ref[:, pl.ds(r0, C), :] = y.reshape(nbat, C, GW)

        for bb in range(nbat):
            g_last = gcum[bb * C + C - 1:bb * C + C, :]
            kdec = (kk[rows[bb]] * jnp.exp2(g_last - gcum[rows[bb]])).astype(BF16)
            st_ref[bb] = st[bb] * jnp.exp2(g_last) + jnp.where(bd, _tn(vv[rows[bb]], kdec), 0.0)
        return carry

    lax.fori_loop(0, s_len // C, step, 0)


def _hgrn(main3, aux3, lb_logits, gain, layer):
    b, s, _ = main3.shape
    nbat = HGRN_BATCH if b % HGRN_BATCH == 0 else 1
    col = lambda cb: pl.BlockSpec((nbat, s, GW), lambda i: (i, 0, cb))
    return pl.pallas_call(
        functools.partial(_hgrn_body, layer=layer),
        grid=(b // nbat,),
        in_specs=[pl.BlockSpec(lb_logits.shape, lambda i: (0, 0)),
                  col(CB_HQ), col(CB_HI), col(CB_HG),
                  pl.BlockSpec((nbat, s, GW), lambda i: (i, 0, 0)),
                  pl.BlockSpec((1, GW), lambda i: (0, 0))],
        out_specs=pl.BlockSpec((nbat, s, GW), lambda i: (i, 0, 0)),
        out_shape=jax.ShapeDtypeStruct((b, s, GW), BF16),
        scratch_shapes=[pltpu.VMEM((nbat, GW, GW), F32)],
        compiler_params=_params("parallel"),
        name="hgrn2",
    )(lb_logits, main3, main3, main3, aux3, gain)


def _bucket_tables():
    def bucket(dist):
        n = np.maximum(dist, 0)
        exact = REL_BUCKETS // 2
        nf = np.maximum(n, 1).astype(np.float32)
        large = exact + (np.log(nf / np.float32(exact)) / np.float32(math.log(REL_MAX_DIST / exact))
                         * np.float32(REL_BUCKETS - exact)).astype(np.int32)
        large = np.minimum(large, REL_BUCKETS - 1)
        return np.where(n < exact, n, large).astype(np.int32)
    tk = np.arange(MOBA_BLOCK)[:, None]
    tq = np.arange(MOBA_BLOCK)[None, :]
    own = np.where(tq - tk >= 0, bucket(tq - tk), REL_BUCKETS)
    adj = bucket(MOBA_BLOCK + tq - tk)
    far = int(bucket(np.array([2 * MOBA_BLOCK]))[0])
    return np.stack([own, adj]).astype(np.int32), far


def _bias_body(rb_ref, bk_ref, o_ref):
    bk = bk_ref[...]
    bs = bk.shape[-1]
    for h in range(HEADS):
        acc = jnp.full(bk.shape, NEG_BIG, F32)
        for j in range(REL_BUCKETS):
            acc = jnp.where(bk == j, rb_ref[j, h] * LOG2E, acc)
        o_ref[:, :, h * bs:(h + 1) * bs] = acc


def _bias_tiles(rel_bias, buckets):
    two, bk, bq = buckets.shape
    return pl.pallas_call(
        _bias_body,
        in_specs=[pl.BlockSpec(memory_space=pltpu.SMEM), pl.BlockSpec(buckets.shape, lambda: (0, 0, 0))],
        out_specs=pl.BlockSpec((two, bk, HEADS * bq), lambda: (0, 0, 0)),
        out_shape=jax.ShapeDtypeStruct((two, bk, HEADS * bq), F32),
        name="moba_bias",
    )(rel_bias, jnp.asarray(buckets))


VT_ROWS = HEAD_DIM + 16


def _moba_body(rb_ref, q_ref, k_ref, v_ref, bias_ref, y_ref, vt_ref, km_ref, qbd_ref, *, far_bucket):
    jp = pl.program_id(1)
    BS = MOBA_BLOCK
    nb = k_ref.shape[1] // BS
    n_sel = min(MOBA_TOPK, nb - 1)
    scale = HEAD_DIM ** -0.5

    @pl.when(jp == 0)
    def _():
        hid = _head_of((1, GW), 1)
        vt_ref[:, HEAD_DIM:, :] = jnp.ones((HEADS, VT_ROWS - HEAD_DIM, nb * BS), BF16)
        for n in range(nb):
            cols = slice(n * BS, (n + 1) * BS)
            vt = jnp.transpose(v_ref[0, cols, :].astype(F32)).astype(BF16)
            km = jnp.mean(k_ref[0, cols, :].astype(F32), axis=0, keepdims=True)
            for h in range(HEADS):
                vt_ref[h, 0:HEAD_DIM, cols] = vt[h * HEAD_DIM:(h + 1) * HEAD_DIM]
                km_ref[h * nb + n:h * nb + n + 1, :] = jnp.where(hid == h, km, 0.0).astype(BF16)

    hid_q = _head_of((BS, GW), 1)
    blk = lax.broadcasted_iota(jnp.int32, (nb, BS), 0)
    sels = []
    for t in range(MOBA_TILES):
        i = MOBA_TILES * jp + t
        q = q_ref[0, t * BS:(t + 1) * BS, :]
        qs = (q.astype(F32) * (scale * LOG2E)).astype(BF16)
        for h in range(HEADS):
            qbd_ref[t, h * BS:(h + 1) * BS, :] = jnp.where(hid_q == h, qs, jnp.zeros_like(qs))
        gate = _nt(km_ref[...], q)
        sel = []
        for h in range(HEADS):
            g = jnp.where(blk < i, gate[h * nb:(h + 1) * nb], NEG_BIG)
            rank = jnp.zeros((nb, BS), jnp.int32)
            for mrow in range(nb):
                gm = g[mrow:mrow + 1]
                ahead = (gm > g) | ((gm == g) & (mrow < blk))
                rank = rank + jnp.where(ahead, 1, 0)
            chosen = (rank < n_sel) & (blk < i)
            far = jnp.where(blk < i - 1, rb_ref[far_bucket, h] * LOG2E, 0.0)
            sel.append(jnp.where(chosen, far, NEG_BIG))
        sels.append(jnp.concatenate(sel, axis=1))

    def attend(pair):
        lanes = [(t, h) for t in range(MOBA_TILES) for h in range(HEADS)]
        past = [MOBA_TILES * pair + t for t in range(MOBA_TILES)]
        s_all = {(t, h): _nt(k_ref[0, 0:(past[t] + 1) * BS, :], qbd_ref[t, h * BS:(h + 1) * BS, :]) for t, h in lanes}
        m = dict.fromkeys(lanes)
        acc = dict.fromkeys(lanes)
        for n in range(max(past) + 1):
            live = [(t, h) for t, h in lanes if n <= past[t]]
            p, alpha = {}, {}
            for t, h in live:
                hs = slice(h * BS, (h + 1) * BS)
                s = s_all[t, h][n * BS:(n + 1) * BS]
                if n == past[t]:
                    s, r = s + bias_ref[0, :, hs], None
                elif n == past[t] - 1:
                    s, r = s + bias_ref[1, :, hs], sels[t][n:n + 1, hs]
                else:
                    r = sels[t][n:n + 1, hs]
                bm = jnp.max(s, axis=0, keepdims=True)
                bm = bm if r is None else bm + r
                m_new = bm if m[t, h] is None else jnp.maximum(m[t, h], bm)
                p[t, h] = jnp.exp2(s - (m_new if r is None else m_new - r)).astype(BF16)
                alpha[t, h] = None if m[t, h] is None else jnp.exp2(m[t, h] - m_new)
                m[t, h] = m_new
            for t, h in live:
                pv = _dot(vt_ref[h, :, n * BS:(n + 1) * BS], p[t, h])
                acc[t, h] = pv if alpha[t, h] is None else alpha[t, h] * acc[t, h] + pv
        for t in range(MOBA_TILES):
            outs = [acc[t, h][0:HEAD_DIM] * (1.0 / acc[t, h][HEAD_DIM:HEAD_DIM + 1]) for h in range(HEADS)]
            y_ref[0, t * BS:(t + 1) * BS, :] = jnp.transpose(jnp.concatenate(outs, axis=0)).astype(y_ref.dtype)

    for pair in range(nb // MOBA_TILES):
        pl.when(jp == pair)(functools.partial(attend, pair))


def _moba(main3, rel_bias, bias_tiles, far_bucket):
    b, s, _ = main3.shape
    nb = s // MOBA_BLOCK
    rows = MOBA_TILES * MOBA_BLOCK
    return pl.pallas_call(
        functools.partial(_moba_body, far_bucket=far_bucket),
        grid=(b, nb // MOBA_TILES),
        in_specs=[pl.BlockSpec(memory_space=pltpu.SMEM),
                  pl.BlockSpec((1, rows, GW), lambda bi, i: (bi, i, CB_AQ)),
                  pl.BlockSpec((1, s, GW), lambda bi, i: (bi, 0, CB_AK)),
                  pl.BlockSpec((1, s, GW), lambda bi, i: (bi, 0, CB_AV)),
                  pl.BlockSpec(bias_tiles.shape, lambda bi, i: (0, 0, 0), pipeline_mode=pl.Buffered(1))],
        out_specs=pl.BlockSpec((1, rows, GW), lambda bi, i: (bi, i, 0)),
        out_shape=jax.ShapeDtypeStruct((b, s, GW), BF16),
        scratch_shapes=[pltpu.VMEM((HEADS, VT_ROWS, s), BF16),
                        pltpu.VMEM((HEADS * nb, GW), BF16),
                        pltpu.VMEM((MOBA_TILES, HEADS * MOBA_BLOCK, GW), BF16)],
        compiler_params=_params("parallel", "arbitrary"),
        name="moba",
    )(rel_bias, main3, main3, main3, bias_tiles)


def _memkv_body(mem_ref, g_ref, wk_ref, wv_ref, k_ref, v_ref):
    mn = _rms(mem_ref[0], g_ref[...]).astype(BF16)
    k_ref[0] = _dot(mn, wk_ref[...]).astype(BF16)
    v_ref[0] = _dot(mn, wv_ref[...]).astype(BF16)


def _memkv(mem, gain, wk, wv):
    b, nm, d = mem.shape
    cw = wk.shape[1]
    const = lambda i: (0, 0)
    return pl.pallas_call(
        _memkv_body,
        grid=(b,),
        in_specs=[pl.BlockSpec((1, nm, d), lambda i: (i, 0, 0)), pl.BlockSpec((1, d), const),
                  pl.BlockSpec(wk.shape, const), pl.BlockSpec(wv.shape, const)],
        out_specs=[pl.BlockSpec((1, nm, cw), lambda i: (i, 0, 0))] * 2,
        out_shape=[jax.ShapeDtypeStruct((b, nm, cw), BF16)] * 2,
        compiler_params=_params("parallel"),
        name="memkv",
    )(mem, gain, wk, wv)


def _mixout_cross_body(ym_ref, yc_ref, ya_ref, yh_ref, wout_ref, gmix_ref, x_ref,
                       gpre_ref, wq_ref, k_ref, v_ref, wo_ref, gpost_ref, o_ref):
    h = (_dot(ym_ref[0], wout_ref[0 * GW:1 * GW, :]) + _dot(yc_ref[0], wout_ref[1 * GW:2 * GW, :])
         + _dot(ya_ref[0], wout_ref[2 * GW:3 * GW, :]) + _dot(yh_ref[0], wout_ref[3 * GW:4 * GW, :]))
    x = x_ref[0] + _rms(h, gmix_ref[...])
    xn = _rms(x, gpre_ref[...]).astype(BF16)
    q = _dot(xn, wq_ref[...]).astype(BF16)
    k = k_ref[0]
    v = v_ref[0]
    outs = []
    for hh in range(CROSS_HEADS):
        sl = slice(hh * CROSS_HEAD_DIM, (hh + 1) * CROSS_HEAD_DIM)
        s = _nt(q[:, sl], k[:, sl]) * (CROSS_HEAD_DIM ** -0.5)
        p = jnp.exp(s - jnp.max(s, axis=-1, keepdims=True))
        inv = 1.0 / jnp.sum(p, axis=-1, keepdims=True)
        outs.append((_dot(p.astype(BF16), v[:, sl]) * inv).astype(BF16))
    o = jnp.concatenate(outs, axis=-1)
    o_ref[0] = x + _rms(_dot(o, wo_ref[...]), gpost_ref[...])


def _mixout_cross(ys, wout, gmix, x3, gpre, wq, km, vm, wo, gpost, tq):
    b, s, d = x3.shape
    nm, cw = km.shape[1:]
    const = lambda bi, i: (0, 0)
    tile = lambda width: pl.BlockSpec((1, tq, width), lambda bi, i: (bi, i, 0))
    mem = pl.BlockSpec((1, nm, cw), lambda bi, i: (bi, 0, 0))
    return pl.pallas_call(
        _mixout_cross_body,
        grid=(b, s // tq),
        in_specs=[tile(GW), tile(GW), tile(GW), tile(GW),
                  pl.BlockSpec(wout.shape, const), pl.BlockSpec((1, d), const), tile(d),
                  pl.BlockSpec((1, d), const), pl.BlockSpec(wq.shape, const), mem, mem,
                  pl.BlockSpec(wo.shape, const), pl.BlockSpec((1, d), const)],
        out_specs=tile(d),
        out_shape=jax.ShapeDtypeStruct((b, s, d), F32),
        compiler_params=_params("parallel", "parallel"),
        name="mixout_cross",
    )(*ys, wout, gmix, x3, gpre, wq, km, vm, wo, gpost)


def _ffn_body(x_ref, xh_ref, gpre_ref, wg_ref, wu_ref, cw_ref, cb_ref, wd_ref, gpost_ref, o_ref,
              xn_ref, act_ref, *, seq_len):
    i = pl.program_id(0)
    tm = x_ref.shape[0]
    g = gpre_ref[...]
    x = x_ref[...]
    xn_ref[8:, :] = _rms(x, g).astype(BF16)
    starts_seq = (i * tm) % seq_len == 0
    xn_ref[0:8, :] = jnp.where(starts_seq, 0.0, _rms(xh_ref[...], g)).astype(BF16)
    xe = xn_ref[...]
    for c in range(0, act_ref.shape[1], FFN_CHUNK):
        cs = slice(c, c + FFN_CHUNK)
        ge = _dot(xe, wg_ref[:, cs])
        up = _dot(xe[8:], wu_ref[:, cs])
        cw = cw_ref[:, cs]
        gate = cw[0:1] * ge[6:6 + tm] + cw[1:2] * ge[7:7 + tm] + cw[2:3] * ge[8:8 + tm] + cb_ref[:, cs]
        act_ref[:, cs] = (jax.nn.silu(gate) * up).astype(BF16)
    o_ref[...] = x + _rms(_dot(act_ref[...], wd_ref[...]), gpost_ref[...])


def _ffn(x2, gpre, wg, wu, cw, cb, wd, gpost, seq_len, tm):
    m, d = x2.shape
    f = wg.shape[1]
    const = lambda i: (0, 0)
    resident = lambda shape: pl.BlockSpec(shape, const, pipeline_mode=pl.Buffered(1))
    return pl.pallas_call(
        functools.partial(_ffn_body, seq_len=seq_len),
        grid=(m // tm,),
        in_specs=[pl.BlockSpec((tm, d), lambda i: (i, 0)),
                  pl.BlockSpec((8, d), lambda i: (jnp.maximum(i * (tm // 8) - 1, 0), 0)),
                  pl.BlockSpec((1, d), const),
                  resident((d, f)), resident((d, f)),
                  pl.BlockSpec((3, f), const), pl.BlockSpec((1, f), const),
                  resident((f, d)),
                  pl.BlockSpec((1, d), const)],
        out_specs=pl.BlockSpec((tm, d), lambda i: (i, 0)),
        out_shape=jax.ShapeDtypeStruct((m, d), F32),
        scratch_shapes=[pltpu.VMEM((tm + 8, d), BF16), pltpu.VMEM((tm, f), BF16)],
        compiler_params=_params("parallel"),
        name="ffn",
    )(x2, x2, gpre, wg, wu, cw, cb, wd, gpost)


def _row_tile(m, pref):
    t = pref
    while m % t:
        t //= 2
    return t


def kernel(x, mem, w_in, b_in, mlstm_norm, sconv_w, rel_bias, hgrn_lb_logits, hgrn_norm, w_mix_out, norm_mix_pre, norm_mix_post, mem_norm, w_cq, w_ck, w_cv, w_co, norm_cross_pre, norm_cross_post, w_ffn_in, ffn_conv_w, ffn_conv_b, w_ffn_out, norm_ffn_pre, norm_ffn_post):
    b, s, d = x.shape
    depth = w_in.shape[0]
    d_ff = w_ffn_out.shape[1]
    assert s % (MOBA_TILES * MOBA_BLOCK) == 0 and s % MLSTM_CHUNK == 0 and s % HGRN_CHUNK == 0
    m = b * s
    tm = _row_tile(m, 512)

    gate0 = 4 * GW
    hq0 = gate0 + 2 * HEADS + 6 * GW
    main_cols = lambda a: jnp.concatenate([a[..., :gate0], a[..., gate0 + 2 * HEADS:hq0 + GW], a[..., hq0 + 2 * GW:]], axis=-1)
    aux_cols = lambda a: jnp.concatenate(
        [a[..., hq0 + GW:hq0 + 2 * GW], a[..., gate0:gate0 + 2 * HEADS],
         jnp.zeros(a.shape[:-1] + (N_AUX - GW - 2 * HEADS,), a.dtype)], axis=-1)
    wm_all = main_cols(w_in).astype(BF16)
    wa_all = aux_cols(w_in).astype(BF16)
    bm_all = main_cols(b_in)[:, None, :]
    ba_all = aux_cols(b_in)[:, None, :]

    w_out = w_mix_out.astype(BF16)
    wq, wk, wv, wo = (w.astype(BF16) for w in (w_cq, w_ck, w_cv, w_co))
    w_gate = w_ffn_in[..., :d_ff].astype(BF16)
    w_up = w_ffn_in[..., d_ff:].astype(BF16)
    w_down = w_ffn_out.astype(BF16)
    row = lambda a, l: a[l][None, :]

    buckets, far_bucket = _bucket_tables()
    bias_tiles = _bias_tiles(rel_bias, buckets)
    assert d_ff % FFN_CHUNK == 0

    x2 = x.reshape(m, d)
    for l in range(depth):
        main, aux, y_c = _inproj(x2, row(norm_mix_pre, l), wm_all[l], bm_all[l], wa_all[l], ba_all[l], sconv_w[l], s, tm)
        y_c = y_c.reshape(b, s, GW)
        main3 = main.reshape(b, s, N_MAIN)
        aux3 = aux.reshape(b, s, N_AUX)
        y_m = _mlstm(main3, aux3, row(mlstm_norm, l))
        y_a = _moba(main3, rel_bias, bias_tiles, far_bucket)
        y_h = _hgrn(main3, aux3, hgrn_lb_logits, row(hgrn_norm, l), l)
        km, vm = _memkv(mem, row(mem_norm, l), wk[l], wv[l])
        x2 = _mixout_cross((y_m, y_c, y_a, y_h), w_out[l], row(norm_mix_post, l), x2.reshape(b, s, d),
                           row(norm_cross_pre, l), wq[l], km, vm, wo[l], row(norm_cross_post, l),
                           _row_tile(s, 512)).reshape(m, d)

        x2 = _ffn(x2, row(norm_ffn_pre, l), w_gate[l], w_up[l], ffn_conv_w[l], row(ffn_conv_b, l),
                  w_down[l], row(norm_ffn_post, l), s, _row_tile(m, FFN_ROWS))
    return x2.reshape(b, s, d)
```

```python
import functools
import math

import numpy as np
import jax
import jax.numpy as jnp
from jax import lax
from jax.experimental import pallas as pl
from jax.experimental.pallas import tpu as pltpu

F32 = jnp.float32
BF16 = jnp.bfloat16

HEADS = 4
HEAD_DIM = 64
GW = HEADS * HEAD_DIM
MOBA_BLOCK = 256
MOBA_TOPK = 3
MOBA_TILES = 2
REL_BUCKETS = 32
REL_MAX_DIST = 128
CROSS_HEADS = 4
CROSS_HEAD_DIM = 128
RMS_EPS = 1e-6
NEG_BIG = -1e30
LOG2E = 1.4426950408889634

MLSTM_CHUNK = 256
MLSTM_BATCH = 2
HGRN_CHUNK = 64
HGRN_BATCH = 2
FFN_CHUNK = 256
FFN_ROWS = 512
VMEM_LIMIT = 56 * 1024 * 1024

(WB_MQ, WB_MK, WB_MV, WB_MO, WB_CB, WB_CC, WB_CH, WB_AQ, WB_AK, WB_AV, WB_HQ, WB_HI, WB_HG) = range(13)
N_W = 13 * GW
(CB_MQ, CB_MK, CB_MV, CB_MO, CB_AQ, CB_AK, CB_AV, CB_HQ, CB_HI, CB_HG) = range(10)
N_MAIN = 10 * GW
N_AUX = 512


def _rms(x, g):
    return x * lax.rsqrt(jnp.mean(x * x, axis=-1, keepdims=True) + RMS_EPS) * g


def _nt(a, b):
    return lax.dot_general(a, b, (((1,), (1,)), ((), ())), preferred_element_type=F32)


def _tn(a, b):
    return lax.dot_general(a, b, (((0,), (0,)), ((), ())), preferred_element_type=F32)


def _dot(a, b):
    return jnp.dot(a, b, preferred_element_type=F32)


def _head_of(shape, axis):
    return lax.shift_right_logical(lax.broadcasted_iota(jnp.int32, shape, axis), 6)


def _params(*sem):
    return pltpu.CompilerParams(dimension_semantics=sem, vmem_limit_bytes=VMEM_LIMIT)


def _inproj_body(x_ref, g_ref, wm_ref, bm_ref, wa_ref, ba_ref, cw_ref, om_ref, oa_ref, oc_ref, halo_ref, *, seq_len):
    i = pl.program_id(0)
    tm = x_ref.shape[0]

    @pl.when(i == 0)
    def _():
        halo_ref[...] = jnp.zeros_like(halo_ref)

    xn = _rms(x_ref[...], g_ref[...]).astype(BF16)

    def proj(wb):
        return _dot(xn, wm_ref[:, wb * GW:(wb + 1) * GW]) + bm_ref[:, wb * GW:(wb + 1) * GW]

    ob = 0
    for wb in range(N_W // GW):
        if wb not in (WB_CB, WB_CC, WB_CH):
            om_ref[:, ob * GW:(ob + 1) * GW] = proj(wb).astype(om_ref.dtype)
            ob += 1
    oa_ref[...] = _dot(xn, wa_ref[...]) + ba_ref[...]

    u = proj(WB_CC) * proj(WB_CH)
    halo = jnp.where((i * tm) % seq_len == 0, 0.0, halo_ref[...])
    ue = jnp.concatenate([halo, u], axis=0)
    w = cw_ref[...]
    y = proj(WB_CB) * (w[0:1] * ue[6:6 + tm] + w[1:2] * ue[7:7 + tm] + w[2:3] * u)
    oc_ref[...] = y.astype(oc_ref.dtype)
    halo_ref[...] = u[tm - 8:tm]


def _inproj(x2, gain, wm, bm, wa, ba, conv_w, layer, seq_len, tm):
    m, d = x2.shape
    const = lambda i: (0, 0)
    of_layer = lambda a: pl.BlockSpec((None,) + a.shape[1:], lambda i: (layer, 0, 0))
    rows = lambda width: pl.BlockSpec((tm, width), lambda i: (i, 0))
    return pl.pallas_call(
        functools.partial(_inproj_body, seq_len=seq_len),
        grid=(m // tm,),
        in_specs=[rows(d), pl.BlockSpec((1, d), const),
                  of_layer(wm), of_layer(bm), of_layer(wa), of_layer(ba), of_layer(conv_w)],
        out_specs=[rows(N_MAIN), rows(N_AUX), rows(GW)],
        out_shape=[jax.ShapeDtypeStruct((m, N_MAIN), BF16),
                   jax.ShapeDtypeStruct((m, N_AUX), F32),
                   jax.ShapeDtypeStruct((m, GW), BF16)],
        scratch_shapes=[pltpu.VMEM((8, GW), F32)],
        compiler_params=_params("arbitrary"),
        name="inproj",
    )(x2, gain, wm, bm, wa, ba, conv_w)


def _scan_lanes(x, op, seg):
    pos = lax.broadcasted_iota(jnp.int32, x.shape, 1) & (seg - 1)
    s = 1
    while s < seg:
        x = jnp.where(pos >= s, op(x, pltpu.roll(x, s, 1)), x)
        s *= 2
    return x


def _head_rms_gate(hval, gain, gate, bones):
    sq = hval * hval
    hi = sq.astype(BF16)
    lo = (sq - hi.astype(F32)).astype(BF16)
    ms = (_dot(hi, bones) + _dot(lo, bones)) * (1.0 / HEAD_DIM)
    return hval * lax.rsqrt(ms + RMS_EPS) * gain * gate


def _block_ones():
    r = _head_of((GW, GW), 0)
    c = _head_of((GW, GW), 1)
    return r == c


def _split3(x):
    hi = x.astype(BF16).astype(F32)
    r = x - hi
    mid = r.astype(BF16).astype(F32)
    lo = (r - mid).astype(BF16).astype(F32)
    return hi, mid, lo


def _mlstm_body(q_ref, k_ref, v_ref, o_ref, g_ref, gain_ref, y_ref, ct_ref, n_ref, m_ref, g_ref2, c_ref2, cm_ref2):
    s_len = q_ref.shape[1]
    L = MLSTM_CHUNK
    iota = lambda shape, axis: lax.broadcasted_iota(jnp.int32, shape, axis)
    bd = _block_ones()
    bones = jnp.where(bd, 1.0, 0.0).astype(BF16)
    hid_l = _head_of((L, GW), 1)
    hmask = [hid_l == h for h in range(HEADS)]
    tril = iota((L, L), 0) >= iota((L, L), 1)
    row8 = iota((8, L), 0)
    row16 = iota((16, L), 0)
    hid1 = _head_of((1, GW), 1)
    gain = gain_ref[...]
    ones_bd = jnp.where(_head_of((HEADS * L, GW), 1) == iota((HEADS * L, GW), 0) // L, 1.0, 0.0).astype(BF16)
    csel = iota((128, 3 * GW), 1)
    esel = jnp.where(iota((128, 3 * GW), 0) == 16 + 4 * (csel // GW) + (csel % GW) // HEAD_DIM, 1.0, 0.0).astype(BF16)
    ones8 = jnp.ones((8, L), BF16)

    ct_ref[...] = jnp.zeros_like(ct_ref)
    n_ref[...] = jnp.zeros_like(n_ref)
    m_ref[...] = jnp.zeros_like(m_ref)

    for bb in range(q_ref.shape[0]):
        gt = jnp.concatenate([jnp.transpose(g_ref[bb, c * L:(c + 1) * L, :])[0:8] for c in range(s_len // L)], axis=1)
        g = _scan_lanes(jax.nn.log_sigmoid(pltpu.roll(gt, 4, 0)), jnp.add, L)
        cc = gt - g
        g_ref2[bb] = g
        c_ref2[bb] = cc
        cm_ref2[bb] = _scan_lanes(cc, jnp.maximum, L)

    def prep(bb, r0):
        q = q_ref[bb, pl.ds(r0, L), :]
        k = (k_ref[bb, pl.ds(r0, L), :].astype(F32) * (HEAD_DIM ** -0.5)).astype(BF16)
        v = v_ref[bb, pl.ds(r0, L), :]
        g = g_ref2[bb, :, pl.ds(r0, L)]
        cc = c_ref2[bb, :, pl.ds(r0, L)]
        mprev = m_ref[bb][:, 0:1]
        pp = jnp.maximum(mprev, cm_ref2[bb, :, pl.ds(r0, L)])
        ein = jnp.exp(mprev - pp)
        p_last = pp[:, L - 1:L]
        ws = jnp.exp(cc - p_last)
        decay = jnp.exp(mprev - p_last)
        m_new = g[:, L - 1:L] + p_last
        emt = jnp.exp(-(g + pp))
        p_hi, p_mid, p_lo = _split3(pp)
        b0 = jnp.where(row8 < 4, p_hi, pltpu.roll(p_mid, 4, 0))
        b1 = jnp.where(row8 < 4, p_lo, jnp.where(row8 < 7, 1.0, 0.0))
        b2 = jnp.where(row8 < 4, ein, pltpu.roll(ws, 4, 0))
        b3 = jnp.where(row8 < 4, emt, 0.0)
        cols = jnp.transpose(jnp.concatenate([b0, b1, b2, b3, jnp.zeros((96, L), F32)], axis=0)).astype(BF16)
        return dict(bb=bb, r0=r0, q=q, k=k, v=v, cpieces=_split3(cc), cols=cols, decay=decay, m_new=m_new, w=[])

    def head_weights(st, h):
        c_hi, c_mid, c_lo = st["cpieces"]
        pick = (row16 == h) | (row16 == 4 + h) | (row16 == 8 + h)
        bh = jnp.where(pick, -1.0, jnp.where(row16 == 12, c_hi[h:h + 1], jnp.where(
            row16 == 13, c_mid[h:h + 1], jnp.where(row16 == 14, c_lo[h:h + 1], 0.0))))
        bh = jnp.concatenate([bh, jnp.zeros((112, L), F32)], axis=0).astype(BF16)
        expo = _dot(st["cols"], bh)
        sc = _nt(jnp.where(hmask[h], st["q"], jnp.zeros_like(st["q"])), st["k"])
        st["w"].append((jnp.exp(jnp.where(tril, expo, NEG_BIG)) * sc).astype(BF16))

    def finish(st):
        bb, r0, q, k, v, cols = st["bb"], st["r0"], st["q"], st["k"], st["v"], st["cols"]
        w_cat = jnp.concatenate(st["w"], axis=1)
        v_bd = jnp.concatenate([jnp.where(hmask[h], v, jnp.zeros_like(v)) for h in range(HEADS)], axis=0)
        num = _dot(w_cat, v_bd)
        den = _dot(w_cat, ones_bd)
        x3 = _dot(cols, esel)
        ein_x, ws_x, emt_x = x3[:, :GW], x3[:, GW:2 * GW], x3[:, 2 * GW:]

        ct = ct_ref[bb]
        nrow = n_ref[bb][0:1]
        nbd = jnp.where(bd, jnp.broadcast_to(nrow, (GW, GW)), 0.0).astype(BF16)
        num = num + ein_x * _nt(q, ct.astype(BF16))
        den = den + ein_x * _nt(q, nbd)
        hval = num / jnp.maximum(jnp.abs(den), emt_x)
        gate = jax.nn.sigmoid(o_ref[bb, pl.ds(r0, L), :].astype(F32))
        y_ref[bb, pl.ds(r0, L), :] = _head_rms_gate(hval, gain, gate, bones).astype(y_ref.dtype)

        decay = st["decay"]
        dec_l = jnp.where(hid1 == 0, decay[0:1], jnp.where(hid1 == 1, decay[1:2], jnp.where(hid1 == 2, decay[2:3], decay[3:4])))
        vw = (v.astype(F32) * ws_x).astype(BF16)
        kw = (k.astype(F32) * ws_x).astype(BF16)
        ct_ref[bb] = ct * dec_l + jnp.where(bd, _tn(vw, k), 0.0)
        n_ref[bb] = jnp.broadcast_to(nrow * dec_l + _dot(ones8, kw)[0:1], n_ref.shape[1:])
        m_ref[bb] = jnp.broadcast_to(st["m_new"], m_ref.shape[1:])

    def step(c, carry):
        r0 = pl.multiple_of(c * L, L)
        states = [prep(bb, r0) for bb in range(q_ref.shape[0])]
        for h in range(HEADS):
            for st in states:
                head_weights(st, h)
        for st in states:
            finish(st)
        return carry

    lax.fori_loop(0, s_len // L, step, 0)


def _mlstm(main3, aux3, gain):
    b, s, _ = main3.shape
    nb = MLSTM_BATCH if b % MLSTM_BATCH == 0 else 1
    col = lambda cb: pl.BlockSpec((nb, s, GW), lambda i: (i, 0, cb))
    return pl.pallas_call(
        _mlstm_body,
        grid=(b // nb,),
        in_specs=[col(CB_MQ), col(CB_MK), col(CB_MV), col(CB_MO),
                  pl.BlockSpec((nb, s, 128), lambda i: (i, 0, 2)),
                  pl.BlockSpec((1, GW), lambda i: (0, 0))],
        out_specs=pl.BlockSpec((nb, s, GW), lambda i: (i, 0, 0)),
        out_shape=jax.ShapeDtypeStruct((b, s, GW), BF16),
        scratch_shapes=[pltpu.VMEM((nb, GW, GW), F32), pltpu.VMEM((nb, 8, GW), F32), pltpu.VMEM((nb, 8, 128), F32)]
        + [pltpu.VMEM((nb, 8, s), F32)] * 3,
        compiler_params=_params("parallel"),
        name="mlstm",
    )(main3, main3, main3, main3, aux3, gain)


def _hgrn_body(lbl_ref, q_ref, i_ref, g_ref, f_ref, gain_ref, y_ref, st_ref, *, layer):
    nbat, s_len = q_ref.shape[0], q_ref.shape[1]
    C = HGRN_CHUNK
    R = nbat * C
    bd = _block_ones()
    bones = jnp.where(bd, 1.0, 0.0).astype(BF16)
    hid_r = _head_of((R, GW), 1)
    hm_b = [jnp.where(hid_r == h, 1.0, 0.0).astype(BF16) for h in range(HEADS)]
    hid_c = _head_of((C, GW), 1)
    hmask = [hid_c == h for h in range(HEADS)]
    gain = gain_ref[...]

    lg = lbl_ref[...]
    ex = jnp.exp(lg - jnp.max(lg, axis=0, keepdims=True))
    soft = ex / jnp.sum(ex, axis=0, keepdims=True)
    lb = jnp.zeros((1, GW), F32)
    for j in range(1, layer + 1):
        lb = lb + soft[j:j + 1]

    tq = lax.broadcasted_iota(jnp.int32, (HEADS * C, C), 0) & (C - 1)
    tk = lax.broadcasted_iota(jnp.int32, (HEADS * C, C), 1)
    level_mask = {}
    bsz = C
    while bsz >= 2:
        half = bsz // 2
        same = (tq // bsz) == (tk // bsz)
        level_mask[bsz] = same & ((tq % bsz) >= half) & ((tk % bsz) < half)
        bsz = half
    eye = tq == tk
    rowpos = lax.broadcasted_iota(jnp.int32, (R, GW), 0) & (C - 1)
    steps = [1 << j for j in range(C.bit_length() - 1)]
    past_start = {sh: rowpos >= sh for sh in steps}
    upper_half = {sh: (rowpos & sh) != 0 for sh in steps}

    st_ref[...] = jnp.zeros_like(st_ref)

    def load(ref, r0):
        return ref[:, pl.ds(r0, C), :].reshape(R, GW)

    def stack_heads(a, bb):
        return jnp.concatenate([(a * hm_b[h])[bb * C:(bb + 1) * C] for h in range(HEADS)], axis=0)

    def step(c, carry):
        r0 = pl.multiple_of(c * C, C)
        f = lb + (1.0 - lb) * jax.nn.sigmoid(load(f_ref, r0))
        lf = jnp.log2(f)
        kk = 1.0 - f
        qq = jax.nn.silu(load(q_ref, r0).astype(F32))
        vv = load(i_ref, r0)
        rows = [slice(bb * C, (bb + 1) * C) for bb in range(nbat)]

        gcum = lf
        for sh in steps:
            gcum = gcum + jnp.where(past_start[sh], pltpu.roll(gcum, sh, 0), 0.0)
        st = [st_ref[bb] for bb in range(nbat)]
        qin = (qq * jnp.exp2(gcum)).astype(BF16)
        out = [_nt(qin[rows[bb]], st[bb].astype(BF16)) for bb in range(nbat)]

        qb, kb = qq.astype(BF16), kk.astype(BF16)
        att = [jnp.where(eye, _nt(stack_heads(qb, bb), kb[rows[bb]]), 0.0) for bb in range(nbat)]
        ge = gcum
        for half in steps:
            qd = (qq * jnp.exp2(jnp.minimum(gcum - pltpu.roll(ge, half, 0), 0.0))).astype(BF16)
            kd = (kk * jnp.exp2(ge - gcum)).astype(BF16)
            att = [jnp.where(level_mask[2 * half], _nt(stack_heads(qd, bb), kd[rows[bb]]), att[bb]) for bb in range(nbat)]
            ge = jnp.where(upper_half[half], ge, pltpu.roll(ge, R - half, 0))
        for bb in range(nbat):
            ab = att[bb].astype(BF16)
            for h in range(HEADS):
                out[bb] = out[bb] + jnp.where(hmask[h], _dot(ab[h * C:(h + 1) * C], vv[rows[bb]]), 0.0)

        gate = jax.nn.silu(load(g_ref, r0).astype(F32))
        y = _head_rms_gate(jnp.concatenate(out, axis=0), gain, gate, bones).astype(y_ref.dtype)
        y_ref[:, pl.ds(r0, C), :] = y.reshape(nbat, C, GW)

        for bb in range(nbat):
            g_last = gcum[bb * C + C - 1:bb * C + C, :]
            kdec = (kk[rows[bb]] * jnp.exp2(g_last - gcum[rows[bb]])).astype(BF16)
            st_ref[bb] = st[bb] * jnp.exp2(g_last) + jnp.where(bd, _tn(vv[rows[bb]], kdec), 0.0)
        return carry

    lax.fori_loop(0, s_len // C, step, 0)


def _hgrn(main3, aux3, lb_logits, gain, layer):
    b, s, _ = main3.shape
    nbat = HGRN_BATCH if b % HGRN_BATCH == 0 else 1
    col = lambda cb: pl.BlockSpec((nbat, s, GW), lambda i: (i, 0, cb))
    return pl.pallas_call(
        functools.partial(_hgrn_body, layer=layer),
        grid=(b // nbat,),
        in_specs=[pl.BlockSpec(lb_logits.shape, lambda i: (0, 0)),
                  col(CB_HQ), col(CB_HI), col(CB_HG),
                  pl.BlockSpec((nbat, s, GW), lambda i: (i, 0, 0)),
                  pl.BlockSpec((1, GW), lambda i: (0, 0))],
        out_specs=pl.BlockSpec((nbat, s, GW), lambda i: (i, 0, 0)),
        out_shape=jax.ShapeDtypeStruct((b, s, GW), BF16),
        scratch_shapes=[pltpu.VMEM((nbat, GW, GW), F32)],
        compiler_params=_params("parallel"),
        name="hgrn2",
    )(lb_logits, main3, main3, main3, aux3, gain)


def _bucket_tables():
    def bucket(dist):
        n = np.maximum(dist, 0)
        exact = REL_BUCKETS // 2
        nf = np.maximum(n, 1).astype(np.float32)
        large = exact + (np.log(nf / np.float32(exact)) / np.float32(math.log(REL_MAX_DIST / exact))
                         * np.float32(REL_BUCKETS - exact)).astype(np.int32)
        large = np.minimum(large, REL_BUCKETS - 1)
        return np.where(n < exact, n, large).astype(np.int32)
    tk = np.arange(MOBA_BLOCK)[:, None]
    tq = np.arange(MOBA_BLOCK)[None, :]
    own = np.where(tq - tk >= 0, bucket(tq - tk), REL_BUCKETS)
    adj = bucket(MOBA_BLOCK + tq - tk)
    far = int(bucket(np.array([2 * MOBA_BLOCK]))[0])
    return np.stack([own, adj]).astype(np.int32), far


def _bias_body(rb_ref, bk_ref, o_ref):
    bk = bk_ref[...]
    bs = bk.shape[-1]
    for h in range(HEADS):
        acc = jnp.full(bk.shape, NEG_BIG, F32)
        for j in range(REL_BUCKETS):
            acc = jnp.where(bk == j, rb_ref[j, h] * LOG2E, acc)
        o_ref[:, :, h * bs:(h + 1) * bs] = acc


def _bias_tiles(rel_bias, buckets):
    two, bk, bq = buckets.shape
    return pl.pallas_call(
        _bias_body,
        in_specs=[pl.BlockSpec(memory_space=pltpu.SMEM), pl.BlockSpec(buckets.shape, lambda: (0, 0, 0))],
        out_specs=pl.BlockSpec((two, bk, HEADS * bq), lambda: (0, 0, 0)),
        out_shape=jax.ShapeDtypeStruct((two, bk, HEADS * bq), F32),
        name="moba_bias",
    )(rel_bias, jnp.asarray(buckets))


VT_ROWS = HEAD_DIM + 16


def _moba_body(rb_ref, q_ref, k_ref, v_ref, bias_ref, y_ref, vt_ref, km_ref, qbd_ref, *, far_bucket):
    jp = pl.program_id(1)
    BS = MOBA_BLOCK
    nb = k_ref.shape[1] // BS
    n_sel = min(MOBA_TOPK, nb - 1)
    scale = HEAD_DIM ** -0.5

    @pl.when(jp == 0)
    def _():
        hid = _head_of((1, GW), 1)
        vt_ref[:, HEAD_DIM:, :] = jnp.ones((HEADS, VT_ROWS - HEAD_DIM, nb * BS), BF16)
        for n in range(nb):
            cols = slice(n * BS, (n + 1) * BS)
            vt = jnp.transpose(v_ref[0, cols, :].astype(F32)).astype(BF16)
            km = jnp.mean(k_ref[0, cols, :].astype(F32), axis=0, keepdims=True)
            for h in range(HEADS):
                vt_ref[h, 0:HEAD_DIM, cols] = vt[h * HEAD_DIM:(h + 1) * HEAD_DIM]
                km_ref[h * nb + n:h * nb + n + 1, :] = jnp.where(hid == h, km, 0.0).astype(BF16)

    hid_q = _head_of((BS, GW), 1)
    blk = lax.broadcasted_iota(jnp.int32, (nb, BS), 0)
    sels = []
    for t in range(MOBA_TILES):
        i = MOBA_TILES * jp + t
        q = q_ref[0, t * BS:(t + 1) * BS, :]
        qs = (q.astype(F32) * (scale * LOG2E)).astype(BF16)
        for h in range(HEADS):
            qbd_ref[t, h * BS:(h + 1) * BS, :] = jnp.where(hid_q == h, qs, jnp.zeros_like(qs))
        gate = _nt(km_ref[...], q)
        sel = []
        for h in range(HEADS):
            g = jnp.where(blk < i, gate[h * nb:(h + 1) * nb], NEG_BIG)
            rank = jnp.zeros((nb, BS), jnp.int32)
            for mrow in range(nb):
                gm = g[mrow:mrow + 1]
                ahead = (gm > g) | ((gm == g) & (mrow < blk))
                rank = rank + jnp.where(ahead, 1, 0)
            chosen = (rank < n_sel) & (blk < i)
            far = jnp.where(blk < i - 1, rb_ref[far_bucket, h] * LOG2E, 0.0)
            sel.append(jnp.where(chosen, far, NEG_BIG))
        sels.append(jnp.concatenate(sel, axis=1))

    def attend(pair):
        lanes = [(t, h) for t in range(MOBA_TILES) for h in range(HEADS)]
        past = [MOBA_TILES * pair + t for t in range(MOBA_TILES)]
        s_all = {(t, h): _nt(k_ref[0, 0:(past[t] + 1) * BS, :], qbd_ref[t, h * BS:(h + 1) * BS, :]) for t, h in lanes}
        m = dict.fromkeys(lanes)
        acc = dict.fromkeys(lanes)
        for n in range(max(past) + 1):
            live = [(t, h) for t, h in lanes if n <= past[t]]
            p, alpha = {}, {}
            for t, h in live:
                hs = slice(h * BS, (h + 1) * BS)
                s = s_all[t, h][n * BS:(n + 1) * BS]
                if n == past[t]:
                    s, r = s + bias_ref[0, :, hs], None
                elif n == past[t] - 1:
                    s, r = s + bias_ref[1, :, hs], sels[t][n:n + 1, hs]
                else:
                    r = sels[t][n:n + 1, hs]
                bm = jnp.max(s, axis=0, keepdims=True)
                bm = bm if r is None else bm + r
                m_new = bm if m[t, h] is None else jnp.maximum(m[t, h], bm)
                p[t, h] = jnp.exp2(s - (m_new if r is None else m_new - r)).astype(BF16)
                alpha[t, h] = None if m[t, h] is None else jnp.exp2(m[t, h] - m_new)
                m[t, h] = m_new
            for t, h in live:
                pv = _dot(vt_ref[h, :, n * BS:(n + 1) * BS], p[t, h])
                acc[t, h] = pv if alpha[t, h] is None else alpha[t, h] * acc[t, h] + pv
        for t in range(MOBA_TILES):
            outs = [acc[t, h][0:HEAD_DIM] * (1.0 / acc[t, h][HEAD_DIM:HEAD_DIM + 1]) for h in range(HEADS)]
            y_ref[0, t * BS:(t + 1) * BS, :] = jnp.transpose(jnp.concatenate(outs, axis=0)).astype(y_ref.dtype)

    for pair in range(nb // MOBA_TILES):
        pl.when(jp == pair)(functools.partial(attend, pair))


def _moba(main3, rel_bias, bias_tiles, far_bucket):
    b, s, _ = main3.shape
    nb = s // MOBA_BLOCK
    rows = MOBA_TILES * MOBA_BLOCK
    return pl.pallas_call(
        functools.partial(_moba_body, far_bucket=far_bucket),
        grid=(b, nb // MOBA_TILES),
        in_specs=[pl.BlockSpec(memory_space=pltpu.SMEM),
                  pl.BlockSpec((1, rows, GW), lambda bi, i: (bi, i, CB_AQ)),
                  pl.BlockSpec((1, s, GW), lambda bi, i: (bi, 0, CB_AK)),
                  pl.BlockSpec((1, s, GW), lambda bi, i: (bi, 0, CB_AV)),
                  pl.BlockSpec(bias_tiles.shape, lambda bi, i: (0, 0, 0), pipeline_mode=pl.Buffered(1))],
        out_specs=pl.BlockSpec((1, rows, GW), lambda bi, i: (bi, i, 0)),
        out_shape=jax.ShapeDtypeStruct((b, s, GW), BF16),
        scratch_shapes=[pltpu.VMEM((HEADS, VT_ROWS, s), BF16),
                        pltpu.VMEM((HEADS * nb, GW), BF16),
                        pltpu.VMEM((MOBA_TILES, HEADS * MOBA_BLOCK, GW), BF16)],
        compiler_params=_params("parallel", "arbitrary"),
        name="moba",
    )(rel_bias, main3, main3, main3, bias_tiles)


def _memkv_body(mem_ref, g_ref, wk_ref, wv_ref, k_ref, v_ref):
    mn = _rms(mem_ref[0], g_ref[...]).astype(BF16)
    k_ref[0] = _dot(mn, wk_ref[...]).astype(BF16)
    v_ref[0] = _dot(mn, wv_ref[...]).astype(BF16)


def _memkv(mem, gain, wk, wv, layer):
    b, nm, d = mem.shape
    cw = wk.shape[-1]
    const = lambda i: (0, 0)
    of_layer = lambda a: pl.BlockSpec((None,) + a.shape[1:], lambda i: (layer, 0, 0))
    return pl.pallas_call(
        _memkv_body,
        grid=(b,),
        in_specs=[pl.BlockSpec((1, nm, d), lambda i: (i, 0, 0)), pl.BlockSpec((1, d), const),
                  of_layer(wk), of_layer(wv)],
        out_specs=[pl.BlockSpec((1, nm, cw), lambda i: (i, 0, 0))] * 2,
        out_shape=[jax.ShapeDtypeStruct((b, nm, cw), BF16)] * 2,
        compiler_params=_params("parallel"),
        name="memkv",
    )(mem, gain, wk, wv)


def _mixout_cross_body(ym_ref, yc_ref, ya_ref, yh_ref, wout_ref, gmix_ref, x_ref,
                       gpre_ref, wq_ref, k_ref, v_ref, wo_ref, gpost_ref, o_ref):
    h = (_dot(ym_ref[0], wout_ref[0 * GW:1 * GW, :]) + _dot(yc_ref[0], wout_ref[1 * GW:2 * GW, :])
         + _dot(ya_ref[0], wout_ref[2 * GW:3 * GW, :]) + _dot(yh_ref[0], wout_ref[3 * GW:4 * GW, :]))
    x = x_ref[0] + _rms(h, gmix_ref[...])
    xn = _rms(x, gpre_ref[...]).astype(BF16)
    q = _dot(xn, wq_ref[...]).astype(BF16)
    k = k_ref[0]
    v = v_ref[0]
    outs = []
    for hh in range(CROSS_HEADS):
        sl = slice(hh * CROSS_HEAD_DIM, (hh + 1) * CROSS_HEAD_DIM)
        s = _nt(q[:, sl], k[:, sl]) * (CROSS_HEAD_DIM ** -0.5)
        p = jnp.exp(s - jnp.max(s, axis=-1, keepdims=True))
        inv = 1.0 / jnp.sum(p, axis=-1, keepdims=True)
        outs.append((_dot(p.astype(BF16), v[:, sl]) * inv).astype(BF16))
    o = jnp.concatenate(outs, axis=-1)
    o_ref[0] = x + _rms(_dot(o, wo_ref[...]), gpost_ref[...])


def _mixout_cross(ys, wout, gmix, x3, gpre, wq, km, vm, wo, gpost, layer, tq):
    b, s, d = x3.shape
    nm, cw = km.shape[1:]
    const = lambda bi, i: (0, 0)
    of_layer = lambda a: pl.BlockSpec((None,) + a.shape[1:], lambda bi, i: (layer, 0, 0))
    tile = lambda width: pl.BlockSpec((1, tq, width), lambda bi, i: (bi, i, 0))
    mem = pl.BlockSpec((1, nm, cw), lambda bi, i: (bi, 0, 0))
    return pl.pallas_call(
        _mixout_cross_body,
        grid=(b, s // tq),
        in_specs=[tile(GW), tile(GW), tile(GW), tile(GW),
                  of_layer(wout), pl.BlockSpec((1, d), const), tile(d),
                  pl.BlockSpec((1, d), const), of_layer(wq), mem, mem,
                  of_layer(wo), pl.BlockSpec((1, d), const)],
        out_specs=tile(d),
        out_shape=jax.ShapeDtypeStruct((b, s, d), F32),
        compiler_params=_params("parallel", "parallel"),
        name="mixout_cross",
    )(*ys, wout, gmix, x3, gpre, wq, km, vm, wo, gpost)


def _ffn_body(x_ref, xh_ref, gpre_ref, wg_ref, wu_ref, cw_ref, cb_ref, wd_ref, gpost_ref, o_ref,
              xn_ref, act_ref, *, seq_len):
    i = pl.program_id(0)
    tm = x_ref.shape[0]
    g = gpre_ref[...]
    x = x_ref[...]
    xn_ref[8:, :] = _rms(x, g).astype(BF16)
    starts_seq = (i * tm) % seq_len == 0
    xn_ref[0:8, :] = jnp.where(starts_seq, 0.0, _rms(xh_ref[...], g)).astype(BF16)
    xe = xn_ref[...]
    for c in range(0, act_ref.shape[1], FFN_CHUNK):
        cs = slice(c, c + FFN_CHUNK)
        ge = _dot(xe, wg_ref[:, cs])
        up = _dot(xe[8:], wu_ref[:, cs])
        cw = cw_ref[:, cs]
        gate = cw[0:1] * ge[6:6 + tm] + cw[1:2] * ge[7:7 + tm] + cw[2:3] * ge[8:8 + tm] + cb_ref[:, cs]
        act_ref[:, cs] = (jax.nn.silu(gate) * up).astype(BF16)
    o_ref[...] = x + _rms(_dot(act_ref[...], wd_ref[...]), gpost_ref[...])


def _ffn(x2, gpre, w_in, cw, cb, wd, gpost, layer, seq_len, tm):
    m, d = x2.shape
    f = wd.shape[1]
    const = lambda i: (0, 0)
    resident = lambda shape, col=0: pl.BlockSpec((None,) + shape, lambda i: (layer, 0, col), pipeline_mode=pl.Buffered(1))
    return pl.pallas_call(
        functools.partial(_ffn_body, seq_len=seq_len),
        grid=(m // tm,),
        in_specs=[pl.BlockSpec((tm, d), lambda i: (i, 0)),
                  pl.BlockSpec((8, d), lambda i: (jnp.maximum(i * (tm // 8) - 1, 0), 0)),
                  pl.BlockSpec((1, d), const),
                  resident((d, f), 0), resident((d, f), 1),
                  pl.BlockSpec((None, 3, f), lambda i: (layer, 0, 0)), pl.BlockSpec((1, f), const),
                  resident((f, d)),
                  pl.BlockSpec((1, d), const)],
        out_specs=pl.BlockSpec((tm, d), lambda i: (i, 0)),
        out_shape=jax.ShapeDtypeStruct((m, d), F32),
        scratch_shapes=[pltpu.VMEM((tm + 8, d), BF16), pltpu.VMEM((tm, f), BF16)],
        compiler_params=_params("parallel"),
        name="ffn",
    )(x2, x2, gpre, w_in, w_in, cw, cb, wd, gpost)


def _row_tile(m, pref):
    t = pref
    while m % t:
        t //= 2
    return t


def kernel(x, mem, w_in, b_in, mlstm_norm, sconv_w, rel_bias, hgrn_lb_logits, hgrn_norm, w_mix_out, norm_mix_pre, norm_mix_post, mem_norm, w_cq, w_ck, w_cv, w_co, norm_cross_pre, norm_cross_post, w_ffn_in, ffn_conv_w, ffn_conv_b, w_ffn_out, norm_ffn_pre, norm_ffn_post):
    b, s, d = x.shape
    depth = w_in.shape[0]
    d_ff = w_ffn_out.shape[1]
    assert s % (MOBA_TILES * MOBA_BLOCK) == 0 and s % MLSTM_CHUNK == 0 and s % HGRN_CHUNK == 0
    m = b * s
    tm = _row_tile(m, 512)

    gate0 = 4 * GW
    hq0 = gate0 + 2 * HEADS + 6 * GW
    main_cols = lambda a: jnp.concatenate([a[..., :gate0], a[..., gate0 + 2 * HEADS:hq0 + GW], a[..., hq0 + 2 * GW:]], axis=-1)
    aux_cols = lambda a: jnp.concatenate(
        [a[..., hq0 + GW:hq0 + 2 * GW], a[..., gate0:gate0 + 2 * HEADS],
         jnp.zeros(a.shape[:-1] + (N_AUX - GW - 2 * HEADS,), a.dtype)], axis=-1)
    wm_all = main_cols(w_in).astype(BF16)
    wa_all = aux_cols(w_in).astype(BF16)
    bm_all = main_cols(b_in)[:, None, :]
    ba_all = aux_cols(b_in)[:, None, :]

    w_out = w_mix_out.astype(BF16)
    wq, wk, wv, wo = (w.astype(BF16) for w in (w_cq, w_ck, w_cv, w_co))
    w_ffn = w_ffn_in.astype(BF16)
    w_down = w_ffn_out.astype(BF16)
    row = lambda a, l: a[l][None, :]

    buckets, far_bucket = _bucket_tables()
    bias_tiles = _bias_tiles(rel_bias, buckets)
    assert d_ff % FFN_CHUNK == 0

    x2 = x.reshape(m, d)
    for l in range(depth):
        main, aux, y_c = _inproj(x2, row(norm_mix_pre, l), wm_all, bm_all, wa_all, ba_all, sconv_w, l, s, tm)
        y_c = y_c.reshape(b, s, GW)
        main3 = main.reshape(b, s, N_MAIN)
        aux3 = aux.reshape(b, s, N_AUX)
        y_m = _mlstm(main3, aux3, row(mlstm_norm, l))
        y_a = _moba(main3, rel_bias, bias_tiles, far_bucket)
        y_h = _hgrn(main3, aux3, hgrn_lb_logits, row(hgrn_norm, l), l)
        km, vm = _memkv(mem, row(mem_norm, l), wk, wv, l)
        x2 = _mixout_cross((y_m, y_c, y_a, y_h), w_out, row(norm_mix_post, l), x2.reshape(b, s, d),
                           row(norm_cross_pre, l), wq, km, vm, wo, row(norm_cross_post, l),
                           l, _row_tile(s, 512)).reshape(m, d)

        x2 = _ffn(x2, row(norm_ffn_pre, l), w_ffn, ffn_conv_w, row(ffn_conv_b, l),
                  w_down, row(norm_ffn_post, l), l, s, _row_tile(m, FFN_ROWS))
    return x2.reshape(b, s, d)
```

```python
import functools
import math

import numpy as np
import jax
import jax.numpy as jnp
from jax import lax
from jax.experimental import pallas as pl
from jax.experimental.pallas import tpu as pltpu

F32 = jnp.float32
BF16 = jnp.bfloat16

HEADS = 4
HEAD_DIM = 64
GW = HEADS * HEAD_DIM
MOBA_BLOCK = 256
MOBA_TOPK = 3
MOBA_TILES = 2
REL_BUCKETS = 32
REL_MAX_DIST = 128
CROSS_HEADS = 4
CROSS_HEAD_DIM = 128
RMS_EPS = 1e-6
NEG_BIG = -1e30
LOG2E = 1.4426950408889634

MLSTM_CHUNK = 256
MLSTM_BATCH = 2
HGRN_CHUNK = 64
HGRN_BATCH = 2
FFN_CHUNK = 256
FFN_ROWS = 512
VMEM_LIMIT = 56 * 1024 * 1024

(WB_MQ, WB_MK, WB_MV, WB_MO, WB_CB, WB_CC, WB_CH, WB_AQ, WB_AK, WB_AV, WB_HQ, WB_HI, WB_HG) = range(13)
N_W = 13 * GW
(CB_MQ, CB_MK, CB_MV, CB_MO, CB_AQ, CB_AK, CB_AV, CB_HQ, CB_HI, CB_HG) = range(10)
N_MAIN = 10 * GW
N_AUX = 512


def _rms(x, g):
    return x * lax.rsqrt(jnp.mean(x * x, axis=-1, keepdims=True) + RMS_EPS) * g


def _nt(a, b):
    return lax.dot_general(a, b, (((1,), (1,)), ((), ())), preferred_element_type=F32)


def _tn(a, b):
    return lax.dot_general(a, b, (((0,), (0,)), ((), ())), preferred_element_type=F32)


def _dot(a, b):
    return jnp.dot(a, b, preferred_element_type=F32)


def _head_of(shape, axis):
    return lax.shift_right_logical(lax.broadcasted_iota(jnp.int32, shape, axis), HEAD_DIM.bit_length() - 1)


def _params(*sem):
    return pltpu.CompilerParams(dimension_semantics=sem, vmem_limit_bytes=VMEM_LIMIT)


def _inproj_body(x_ref, g_ref, wm_ref, bm_ref, wa_ref, ba_ref, cw_ref, om_ref, oa_ref, oc_ref, halo_ref, *, seq_len):
    i = pl.program_id(0)
    tm = x_ref.shape[0]

    @pl.when(i == 0)
    def _():
        halo_ref[...] = jnp.zeros_like(halo_ref)

    xn = _rms(x_ref[...], g_ref[...]).astype(BF16)

    def proj(wb):
        return _dot(xn, wm_ref[:, wb * GW:(wb + 1) * GW]) + bm_ref[:, wb * GW:(wb + 1) * GW]

    ob = 0
    for wb in range(N_W // GW):
        if wb not in (WB_CB, WB_CC, WB_CH):
            om_ref[:, ob * GW:(ob + 1) * GW] = proj(wb).astype(om_ref.dtype)
            ob += 1
    oa_ref[...] = _dot(xn, wa_ref[...]) + ba_ref[...]

    u = proj(WB_CC) * proj(WB_CH)
    halo = jnp.where((i * tm) % seq_len == 0, 0.0, halo_ref[...])
    ue = jnp.concatenate([halo, u], axis=0)
    w = cw_ref[...]
    y = proj(WB_CB) * (w[0:1] * ue[6:6 + tm] + w[1:2] * ue[7:7 + tm] + w[2:3] * u)
    oc_ref[...] = y.astype(oc_ref.dtype)
    halo_ref[...] = u[tm - 8:tm]


def _inproj(x2, gain, wm, bm, wa, ba, conv_w, layer, seq_len, tm):
    m, d = x2.shape
    const = lambda i: (0, 0)
    of_layer = lambda a: pl.BlockSpec((None,) + a.shape[1:], lambda i: (layer, 0, 0))
    rows = lambda width: pl.BlockSpec((tm, width), lambda i: (i, 0))
    return pl.pallas_call(
        functools.partial(_inproj_body, seq_len=seq_len),
        grid=(m // tm,),
        in_specs=[rows(d), pl.BlockSpec((1, d), const),
                  of_layer(wm), of_layer(bm), of_layer(wa), of_layer(ba), of_layer(conv_w)],
        out_specs=[rows(N_MAIN), rows(N_AUX), rows(GW)],
        out_shape=[jax.ShapeDtypeStruct((m, N_MAIN), BF16),
                   jax.ShapeDtypeStruct((m, N_AUX), F32),
                   jax.ShapeDtypeStruct((m, GW), BF16)],
        scratch_shapes=[pltpu.VMEM((8, GW), F32)],
        compiler_params=_params("arbitrary"),
        name="inproj",
    )(x2, gain, wm, bm, wa, ba, conv_w)


def _scan_lanes(x, op, seg):
    pos = lax.broadcasted_iota(jnp.int32, x.shape, 1) & (seg - 1)
    s = 1
    while s < seg:
        x = jnp.where(pos >= s, op(x, pltpu.roll(x, s, 1)), x)
        s *= 2
    return x


def _head_rms_gate(hval, gain, gate, bones):
    sq = hval * hval
    hi = sq.astype(BF16)
    lo = (sq - hi.astype(F32)).astype(BF16)
    ms = (_dot(hi, bones) + _dot(lo, bones)) * (1.0 / HEAD_DIM)
    return hval * lax.rsqrt(ms + RMS_EPS) * gain * gate


def _block_ones():
    r = _head_of((GW, GW), 0)
    c = _head_of((GW, GW), 1)
    return r == c


def _split3(x):
    hi = x.astype(BF16).astype(F32)
    r = x - hi
    mid = r.astype(BF16).astype(F32)
    lo = (r - mid).astype(BF16).astype(F32)
    return hi, mid, lo


def _mlstm_body(q_ref, k_ref, v_ref, o_ref, g_ref, gain_ref, y_ref, ct_ref, n_ref, m_ref, g_ref2, c_ref2, cm_ref2):
    s_len = q_ref.shape[1]
    L = MLSTM_CHUNK
    iota = lambda shape, axis: lax.broadcasted_iota(jnp.int32, shape, axis)
    bd = _block_ones()
    bones = jnp.where(bd, 1.0, 0.0).astype(BF16)
    hid_l = _head_of((L, GW), 1)
    hmask = [hid_l == h for h in range(HEADS)]
    tril = iota((L, L), 0) >= iota((L, L), 1)
    row8 = iota((8, L), 0)
    row16 = iota((16, L), 0)
    hid1 = _head_of((1, GW), 1)
    gain = gain_ref[...]
    ones_bd = jnp.where(_head_of((HEADS * L, GW), 1) == iota((HEADS * L, GW), 0) // L, 1.0, 0.0).astype(BF16)
    csel = iota((128, 3 * GW), 1)
    esel = jnp.where(iota((128, 3 * GW), 0) == 16 + 4 * (csel // GW) + (csel % GW) // HEAD_DIM, 1.0, 0.0).astype(BF16)
    ones8 = jnp.ones((8, L), BF16)

    ct_ref[...] = jnp.zeros_like(ct_ref)
    n_ref[...] = jnp.zeros_like(n_ref)
    m_ref[...] = jnp.zeros_like(m_ref)

    for bb in range(q_ref.shape[0]):
        gt = jnp.concatenate([jnp.transpose(g_ref[bb, c * L:(c + 1) * L, :])[0:8] for c in range(s_len // L)], axis=1)
        g = _scan_lanes(jax.nn.log_sigmoid(pltpu.roll(gt, 4, 0)), jnp.add, L)
        cc = gt - g
        g_ref2[bb] = g
        c_ref2[bb] = cc
        cm_ref2[bb] = _scan_lanes(cc, jnp.maximum, L)

    def prep(bb, r0):
        q = q_ref[bb, pl.ds(r0, L), :]
        k = (k_ref[bb, pl.ds(r0, L), :].astype(F32) * (HEAD_DIM ** -0.5)).astype(BF16)
        v = v_ref[bb, pl.ds(r0, L), :]
        g = g_ref2[bb, :, pl.ds(r0, L)]
        cc = c_ref2[bb, :, pl.ds(r0, L)]
        mprev = m_ref[bb][:, 0:1]
        pp = jnp.maximum(mprev, cm_ref2[bb, :, pl.ds(r0, L)])
        ein = jnp.exp(mprev - pp)
        p_last = pp[:, L - 1:L]
        ws = jnp.exp(cc - p_last)
        decay = jnp.exp(mprev - p_last)
        m_new = g[:, L - 1:L] + p_last
        emt = jnp.exp(-(g + pp))
        p_hi, p_mid, p_lo = _split3(pp)
        b0 = jnp.where(row8 < 4, p_hi, pltpu.roll(p_mid, 4, 0))
        b1 = jnp.where(row8 < 4, p_lo, jnp.where(row8 < 7, 1.0, 0.0))
        b2 = jnp.where(row8 < 4, ein, pltpu.roll(ws, 4, 0))
        b3 = jnp.where(row8 < 4, emt, 0.0)
        cols = jnp.transpose(jnp.concatenate([b0, b1, b2, b3, jnp.zeros((96, L), F32)], axis=0)).astype(BF16)
        return dict(bb=bb, r0=r0, q=q, k=k, v=v, cpieces=_split3(cc), cols=cols, decay=decay, m_new=m_new, w=[])

    def head_weights(st, h):
        c_hi, c_mid, c_lo = st["cpieces"]
        pick = (row16 == h) | (row16 == 4 + h) | (row16 == 8 + h)
        bh = jnp.where(pick, -1.0, jnp.where(row16 == 12, c_hi[h:h + 1], jnp.where(
            row16 == 13, c_mid[h:h + 1], jnp.where(row16 == 14, c_lo[h:h + 1], 0.0))))
        bh = jnp.concatenate([bh, jnp.zeros((112, L), F32)], axis=0).astype(BF16)
        expo = _dot(st["cols"], bh)
        sc = _nt(jnp.where(hmask[h], st["q"], jnp.zeros_like(st["q"])), st["k"])
        st["w"].append((jnp.exp(jnp.where(tril, expo, NEG_BIG)) * sc).astype(BF16))

    def finish(st):
        bb, r0, q, k, v, cols = st["bb"], st["r0"], st["q"], st["k"], st["v"], st["cols"]
        w_cat = jnp.concatenate(st["w"], axis=1)
        v_bd = jnp.concatenate([jnp.where(hmask[h], v, jnp.zeros_like(v)) for h in range(HEADS)], axis=0)
        num = _dot(w_cat, v_bd)
        den = _dot(w_cat, ones_bd)
        x3 = _dot(cols, esel)
        ein_x, ws_x, emt_x = x3[:, :GW], x3[:, GW:2 * GW], x3[:, 2 * GW:]

        ct = ct_ref[bb]
        nrow = n_ref[bb][0:1]
        nbd = jnp.where(bd, jnp.broadcast_to(nrow, (GW, GW)), 0.0).astype(BF16)
        num = num + ein_x * _nt(q, ct.astype(BF16))
        den = den + ein_x * _nt(q, nbd)
        hval = num / jnp.maximum(jnp.abs(den), emt_x)
        gate = jax.nn.sigmoid(o_ref[bb, pl.ds(r0, L), :].astype(F32))
        y_ref[bb, pl.ds(r0, L), :] = _head_rms_gate(hval, gain, gate, bones).astype(y_ref.dtype)

        decay = st["decay"]
        dec_l = jnp.where(hid1 == 0, decay[0:1], jnp.where(hid1 == 1, decay[1:2], jnp.where(hid1 == 2, decay[2:3], decay[3:4])))
        vw = (v.astype(F32) * ws_x).astype(BF16)
        kw = (k.astype(F32) * ws_x).astype(BF16)
        ct_ref[bb] = ct * dec_l + jnp.where(bd, _tn(vw, k), 0.0)
        n_ref[bb] = jnp.broadcast_to(nrow * dec_l + _dot(ones8, kw)[0:1], n_ref.shape[1:])
        m_ref[bb] = jnp.broadcast_to(st["m_new"], m_ref.shape[1:])

    def step(c, carry):
        r0 = pl.multiple_of(c * L, L)
        states = [prep(bb, r0) for bb in range(q_ref.shape[0])]
        for h in range(HEADS):
            for st in states:
                head_weights(st, h)
        for st in states:
            finish(st)
        return carry

    lax.fori_loop(0, s_len // L, step, 0)


def _mlstm(main3, aux3, gain):
    b, s, _ = main3.shape
    nb = MLSTM_BATCH if b % MLSTM_BATCH == 0 else 1
    col = lambda cb: pl.BlockSpec((nb, s, GW), lambda i: (i, 0, cb))
    return pl.pallas_call(
        _mlstm_body,
        grid=(b // nb,),
        in_specs=[col(CB_MQ), col(CB_MK), col(CB_MV), col(CB_MO),
                  pl.BlockSpec((nb, s, 128), lambda i: (i, 0, 2)),
                  pl.BlockSpec((1, GW), lambda i: (0, 0))],
        out_specs=pl.BlockSpec((nb, s, GW), lambda i: (i, 0, 0)),
        out_shape=jax.ShapeDtypeStruct((b, s, GW), BF16),
        scratch_shapes=[pltpu.VMEM((nb, GW, GW), F32), pltpu.VMEM((nb, 8, GW), F32), pltpu.VMEM((nb, 8, 128), F32)]
        + [pltpu.VMEM((nb, 8, s), F32)] * 3,
        compiler_params=_params("parallel"),
        name="mlstm",
    )(main3, main3, main3, main3, aux3, gain)


def _hgrn_body(lbl_ref, q_ref, i_ref, g_ref, f_ref, gain_ref, y_ref, st_ref, *, layer):
    nbat, s_len = q_ref.shape[0], q_ref.shape[1]
    C = HGRN_CHUNK
    R = nbat * C
    bd = _block_ones()
    bones = jnp.where(bd, 1.0, 0.0).astype(BF16)
    hid_r = _head_of((R, GW), 1)
    hm_b = [jnp.where(hid_r == h, 1.0, 0.0).astype(BF16) for h in range(HEADS)]
    hid_c = _head_of((C, GW), 1)
    hmask = [hid_c == h for h in range(HEADS)]
    gain = gain_ref[...]

    lg = lbl_ref[...]
    ex = jnp.exp(lg - jnp.max(lg, axis=0, keepdims=True))
    soft = ex / jnp.sum(ex, axis=0, keepdims=True)
    lb = jnp.zeros((1, GW), F32)
    for j in range(1, layer + 1):
        lb = lb + soft[j:j + 1]

    tq = lax.broadcasted_iota(jnp.int32, (HEADS * C, C), 0) & (C - 1)
    tk = lax.broadcasted_iota(jnp.int32, (HEADS * C, C), 1)
    level_mask = {}
    bsz = C
    while bsz >= 2:
        half = bsz // 2
        same = (tq // bsz) == (tk // bsz)
        level_mask[bsz] = same & ((tq % bsz) >= half) & ((tk % bsz) < half)
        bsz = half
    eye = tq == tk
    rowpos = lax.broadcasted_iota(jnp.int32, (R, GW), 0) & (C - 1)
    steps = [1 << j for j in range(C.bit_length() - 1)]
    past_start = {sh: rowpos >= sh for sh in steps}
    upper_half = {sh: (rowpos & sh) != 0 for sh in steps}

    st_ref[...] = jnp.zeros_like(st_ref)

    def load(ref, r0):
        return ref[:, pl.ds(r0, C), :].reshape(R, GW)

    def stack_heads(a, bb):
        return jnp.concatenate([(a * hm_b[h])[bb * C:(bb + 1) * C] for h in range(HEADS)], axis=0)

    def step(c, carry):
        r0 = pl.multiple_of(c * C, C)
        f = lb + (1.0 - lb) * jax.nn.sigmoid(load(f_ref, r0))
        lf = jnp.log2(f)
        kk = 1.0 - f
        qq = jax.nn.silu(load(q_ref, r0).astype(F32))
        vv = load(i_ref, r0)
        rows = [slice(bb * C, (bb + 1) * C) for bb in range(nbat)]

        gcum = lf
        for sh in steps:
            gcum = gcum + jnp.where(past_start[sh], pltpu.roll(gcum, sh, 0), 0.0)
        st = [st_ref[bb] for bb in range(nbat)]
        qin = (qq * jnp.exp2(gcum)).astype(BF16)
        out = [_nt(qin[rows[bb]], st[bb].astype(BF16)) for bb in range(nbat)]

        qb, kb = qq.astype(BF16), kk.astype(BF16)
        att = [jnp.where(eye, _nt(stack_heads(qb, bb), kb[rows[bb]]), 0.0) for bb in range(nbat)]
        ge = gcum
        for half in steps:
            qd = (qq * jnp.exp2(jnp.minimum(gcum - pltpu.roll(ge, half, 0), 0.0))).astype(BF16)
            kd = (kk * jnp.exp2(ge - gcum)).astype(BF16)
            att = [jnp.where(level_mask[2 * half], _nt(stack_heads(qd, bb), kd[rows[bb]]), att[bb]) for bb in range(nbat)]
            ge = jnp.where(upper_half[half], ge, pltpu.roll(ge, R - half, 0))
        for bb in range(nbat):
            ab = att[bb].astype(BF16)
            for h in range(HEADS):
                out[bb] = out[bb] + jnp.where(hmask[h], _dot(ab[h * C:(h + 1) * C], vv[rows[bb]]), 0.0)

        gate = jax.nn.silu(load(g_ref, r0).astype(F32))
        y = _head_rms_gate(jnp.concatenate(out, axis=0), gain, gate, bones).astype(y_ref.dtype)
        y_ref[:, pl.ds(r0, C), :] = y.reshape(nbat, C, GW)

        for bb in range(nbat):
            g_last = gcum[bb * C + C - 1:bb * C + C, :]
            kdec = (kk[rows[bb]] * jnp.exp2(g_last - gcum[rows[bb]])).astype(BF16)
            st_ref[bb] = st[bb] * jnp.exp2(g_last) + jnp.where(bd, _tn(vv[rows[bb]], kdec), 0.0)
        return carry

    lax.fori_loop(0, s_len // C, step, 0)


def _hgrn(main3, aux3, lb_logits, gain, layer):
    b, s, _ = main3.shape
    nbat = HGRN_BATCH if b % HGRN_BATCH == 0 else 1
    col = lambda cb: pl.BlockSpec((nbat, s, GW), lambda i: (i, 0, cb))
    return pl.pallas_call(
        functools.partial(_hgrn_body, layer=layer),
        grid=(b // nbat,),
        in_specs=[pl.BlockSpec(lb_logits.shape, lambda i: (0, 0)),
                  col(CB_HQ), col(CB_HI), col(CB_HG),
                  pl.BlockSpec((nbat, s, GW), lambda i: (i, 0, 0)),
                  pl.BlockSpec((1, GW), lambda i: (0, 0))],
        out_specs=pl.BlockSpec((nbat, s, GW), lambda i: (i, 0, 0)),
        out_shape=jax.ShapeDtypeStruct((b, s, GW), BF16),
        scratch_shapes=[pltpu.VMEM((nbat, GW, GW), F32)],
        compiler_params=_params("parallel"),
        name="hgrn2",
    )(lb_logits, main3, main3, main3, aux3, gain)


def _bucket_tables():
    def bucket(dist):
        n = np.maximum(dist, 0)
        exact = REL_BUCKETS // 2
        nf = np.maximum(n, 1).astype(np.float32)
        large = exact + (np.log(nf / np.float32(exact)) / np.float32(math.log(REL_MAX_DIST / exact))
                         * np.float32(REL_BUCKETS - exact)).astype(np.int32)
        large = np.minimum(large, REL_BUCKETS - 1)
        return np.where(n < exact, n, large).astype(np.int32)
    tk = np.arange(MOBA_BLOCK)[:, None]
    tq = np.arange(MOBA_BLOCK)[None, :]
    own = np.where(tq - tk >= 0, bucket(tq - tk), REL_BUCKETS)
    adj = bucket(MOBA_BLOCK + tq - tk)
    far = int(bucket(np.array([2 * MOBA_BLOCK]))[0])
    return np.stack([own, adj]).astype(np.int32), far


def _bias_body(rb_ref, bk_ref, o_ref):
    bk = bk_ref[...]
    bs = bk.shape[-1]
    for h in range(HEADS):
        acc = jnp.full(bk.shape, NEG_BIG, F32)
        for j in range(REL_BUCKETS):
            acc = jnp.where(bk == j, rb_ref[j, h] * LOG2E, acc)
        o_ref[:, :, h * bs:(h + 1) * bs] = acc


def _bias_tiles(rel_bias, buckets):
    two, bk, bq = buckets.shape
    return pl.pallas_call(
        _bias_body,
        in_specs=[pl.BlockSpec(memory_space=pltpu.SMEM), pl.BlockSpec(buckets.shape, lambda: (0, 0, 0))],
        out_specs=pl.BlockSpec((two, bk, HEADS * bq), lambda: (0, 0, 0)),
        out_shape=jax.ShapeDtypeStruct((two, bk, HEADS * bq), F32),
        name="moba_bias",
    )(rel_bias, jnp.asarray(buckets))


VT_ROWS = HEAD_DIM + 16


def _moba_body(rb_ref, q_ref, k_ref, v_ref, bias_ref, y_ref, vt_ref, km_ref, qbd_ref, *, far_bucket):
    jp = pl.program_id(1)
    BS = MOBA_BLOCK
    nb = k_ref.shape[1] // BS
    n_sel = min(MOBA_TOPK, nb - 1)
    scale = HEAD_DIM ** -0.5

    @pl.when(jp == 0)
    def _():
        hid = _head_of((1, GW), 1)
        vt_ref[:, HEAD_DIM:, :] = jnp.ones((HEADS, VT_ROWS - HEAD_DIM, nb * BS), BF16)
        for n in range(nb):
            cols = slice(n * BS, (n + 1) * BS)
            vt = jnp.transpose(v_ref[0, cols, :].astype(F32)).astype(BF16)
            km = jnp.mean(k_ref[0, cols, :].astype(F32), axis=0, keepdims=True)
            for h in range(HEADS):
                vt_ref[h, 0:HEAD_DIM, cols] = vt[h * HEAD_DIM:(h + 1) * HEAD_DIM]
                km_ref[h * nb + n:h * nb + n + 1, :] = jnp.where(hid == h, km, 0.0).astype(BF16)

    hid_q = _head_of((BS, GW), 1)
    blk = lax.broadcasted_iota(jnp.int32, (nb, BS), 0)
    sels = []
    for t in range(MOBA_TILES):
        i = MOBA_TILES * jp + t
        q = q_ref[0, t * BS:(t + 1) * BS, :]
        qs = (q.astype(F32) * (scale * LOG2E)).astype(BF16)
        for h in range(HEADS):
            qbd_ref[t, h * BS:(h + 1) * BS, :] = jnp.where(hid_q == h, qs, jnp.zeros_like(qs))
        gate = _nt(km_ref[...], q)
        sel = []
        for h in range(HEADS):
            g = jnp.where(blk < i, gate[h * nb:(h + 1) * nb], NEG_BIG)
            rank = jnp.zeros((nb, BS), jnp.int32)
            for mrow in range(nb):
                gm = g[mrow:mrow + 1]
                ahead = (gm > g) | ((gm == g) & (mrow < blk))
                rank = rank + jnp.where(ahead, 1, 0)
            chosen = (rank < n_sel) & (blk < i)
            far = jnp.where(blk < i - 1, rb_ref[far_bucket, h] * LOG2E, 0.0)
            sel.append(jnp.where(chosen, far, NEG_BIG))
        sels.append(jnp.concatenate(sel, axis=1))

    def attend(pair):
        lanes = [(t, h) for t in range(MOBA_TILES) for h in range(HEADS)]
        past = [MOBA_TILES * pair + t for t in range(MOBA_TILES)]
        s_all = {(t, h): _nt(k_ref[0, 0:(past[t] + 1) * BS, :], qbd_ref[t, h * BS:(h + 1) * BS, :]) for t, h in lanes}
        m = dict.fromkeys(lanes)
        acc = dict.fromkeys(lanes)
        for n in range(max(past) + 1):
            live = [(t, h) for t, h in lanes if n <= past[t]]
            p, alpha = {}, {}
            for t, h in live:
                hs = slice(h * BS, (h + 1) * BS)
                s = s_all[t, h][n * BS:(n + 1) * BS]
                if n == past[t]:
                    s, r = s + bias_ref[0, :, hs], None
                elif n == past[t] - 1:
                    s, r = s + bias_ref[1, :, hs], sels[t][n:n + 1, hs]
                else:
                    r = sels[t][n:n + 1, hs]
                bm = jnp.max(s, axis=0, keepdims=True)
                bm = bm if r is None else bm + r
                m_new = bm if m[t, h] is None else jnp.maximum(m[t, h], bm)
                p[t, h] = jnp.exp2(s - (m_new if r is None else m_new - r)).astype(BF16)
                alpha[t, h] = None if m[t, h] is None else jnp.exp2(m[t, h] - m_new)
                m[t, h] = m_new
            for t, h in live:
                pv = _dot(vt_ref[h, :, n * BS:(n + 1) * BS], p[t, h])
                acc[t, h] = pv if alpha[t, h] is None else alpha[t, h] * acc[t, h] + pv
        for t in range(MOBA_TILES):
            outs = [acc[t, h][0:HEAD_DIM] * (1.0 / acc[t, h][HEAD_DIM:HEAD_DIM + 1]) for h in range(HEADS)]
            y_ref[0, t * BS:(t + 1) * BS, :] = jnp.transpose(jnp.concatenate(outs, axis=0)).astype(y_ref.dtype)

    for pair in range(nb // MOBA_TILES):
        pl.when(jp == pair)(functools.partial(attend, pair))


def _moba(main3, rel_bias, bias_tiles, far_bucket):
    b, s, _ = main3.shape
    nb = s // MOBA_BLOCK
    rows = MOBA_TILES * MOBA_BLOCK
    return pl.pallas_call(
        functools.partial(_moba_body, far_bucket=far_bucket),
        grid=(b, nb // MOBA_TILES),
        in_specs=[pl.BlockSpec(memory_space=pltpu.SMEM),
                  pl.BlockSpec((1, rows, GW), lambda bi, i: (bi, i, CB_AQ)),
                  pl.BlockSpec((1, s, GW), lambda bi, i: (bi, 0, CB_AK)),
                  pl.BlockSpec((1, s, GW), lambda bi, i: (bi, 0, CB_AV)),
                  pl.BlockSpec(bias_tiles.shape, lambda bi, i: (0, 0, 0), pipeline_mode=pl.Buffered(1))],
        out_specs=pl.BlockSpec((1, rows, GW), lambda bi, i: (bi, i, 0)),
        out_shape=jax.ShapeDtypeStruct((b, s, GW), BF16),
        scratch_shapes=[pltpu.VMEM((HEADS, VT_ROWS, s), BF16),
                        pltpu.VMEM((HEADS * nb, GW), BF16),
                        pltpu.VMEM((MOBA_TILES, HEADS * MOBA_BLOCK, GW), BF16)],
        compiler_params=_params("parallel", "arbitrary"),
        name="moba",
    )(rel_bias, main3, main3, main3, bias_tiles)


def _memkv_body(mem_ref, g_ref, wk_ref, wv_ref, k_ref, v_ref):
    mn = _rms(mem_ref[0], g_ref[...]).astype(BF16)
    k_ref[0] = _dot(mn, wk_ref[...]).astype(BF16)
    v_ref[0] = _dot(mn, wv_ref[...]).astype(BF16)


def _memkv(mem, gain, wk, wv, layer):
    b, nm, d = mem.shape
    cw = wk.shape[-1]
    const = lambda i: (0, 0)
    of_layer = lambda a: pl.BlockSpec((None,) + a.shape[1:], lambda i: (layer, 0, 0))
    return pl.pallas_call(
        _memkv_body,
        grid=(b,),
        in_specs=[pl.BlockSpec((1, nm, d), lambda i: (i, 0, 0)), pl.BlockSpec((1, d), const),
                  of_layer(wk), of_layer(wv)],
        out_specs=[pl.BlockSpec((1, nm, cw), lambda i: (i, 0, 0))] * 2,
        out_shape=[jax.ShapeDtypeStruct((b, nm, cw), BF16)] * 2,
        compiler_params=_params("parallel"),
        name="memkv",
    )(mem, gain, wk, wv)


def _mixout_cross_body(ym_ref, yc_ref, ya_ref, yh_ref, wout_ref, gmix_ref, x_ref,
                       gpre_ref, wq_ref, k_ref, v_ref, wo_ref, gpost_ref, o_ref):
    h = (_dot(ym_ref[0], wout_ref[0 * GW:1 * GW, :]) + _dot(yc_ref[0], wout_ref[1 * GW:2 * GW, :])
         + _dot(ya_ref[0], wout_ref[2 * GW:3 * GW, :]) + _dot(yh_ref[0], wout_ref[3 * GW:4 * GW, :]))
    x = x_ref[0] + _rms(h, gmix_ref[...])
    xn = _rms(x, gpre_ref[...]).astype(BF16)
    q = _dot(xn, wq_ref[...]).astype(BF16)
    k = k_ref[0]
    v = v_ref[0]
    outs = []
    for hh in range(CROSS_HEADS):
        sl = slice(hh * CROSS_HEAD_DIM, (hh + 1) * CROSS_HEAD_DIM)
        s = _nt(q[:, sl], k[:, sl]) * (CROSS_HEAD_DIM ** -0.5)
        p = jnp.exp(s - jnp.max(s, axis=-1, keepdims=True))
        inv = 1.0 / jnp.sum(p, axis=-1, keepdims=True)
        outs.append((_dot(p.astype(BF16), v[:, sl]) * inv).astype(BF16))
    o = jnp.concatenate(outs, axis=-1)
    o_ref[0] = x + _rms(_dot(o, wo_ref[...]), gpost_ref[...])


def _mixout_cross(ys, wout, gmix, x3, gpre, wq, km, vm, wo, gpost, layer, tq):
    b, s, d = x3.shape
    nm, cw = km.shape[1:]
    const = lambda bi, i: (0, 0)
    of_layer = lambda a: pl.BlockSpec((None,) + a.shape[1:], lambda bi, i: (layer, 0, 0))
    tile = lambda width: pl.BlockSpec((1, tq, width), lambda bi, i: (bi, i, 0))
    mem = pl.BlockSpec((1, nm, cw), lambda bi, i: (bi, 0, 0))
    return pl.pallas_call(
        _mixout_cross_body,
        grid=(b, s // tq),
        in_specs=[tile(GW), tile(GW), tile(GW), tile(GW),
                  of_layer(wout), pl.BlockSpec((1, d), const), tile(d),
                  pl.BlockSpec((1, d), const), of_layer(wq), mem, mem,
                  of_layer(wo), pl.BlockSpec((1, d), const)],
        out_specs=tile(d),
        out_shape=jax.ShapeDtypeStruct((b, s, d), F32),
        compiler_params=_params("parallel", "parallel"),
        name="mixout_cross",
    )(*ys, wout, gmix, x3, gpre, wq, km, vm, wo, gpost)


def _ffn_body(x_ref, xh_ref, gpre_ref, wg_ref, wu_ref, cw_ref, cb_ref, wd_ref, gpost_ref, o_ref,
              xn_ref, act_ref, *, seq_len):
    i = pl.program_id(0)
    tm = x_ref.shape[0]
    g = gpre_ref[...]
    x = x_ref[...]
    xn_ref[8:, :] = _rms(x, g).astype(BF16)
    starts_seq = (i * tm) % seq_len == 0
    xn_ref[0:8, :] = jnp.where(starts_seq, 0.0, _rms(xh_ref[...], g)).astype(BF16)
    xe = xn_ref[...]
    for c in range(0, act_ref.shape[1], FFN_CHUNK):
        cs = slice(c, c + FFN_CHUNK)
        ge = _dot(xe, wg_ref[:, cs])
        up = _dot(xe[8:], wu_ref[:, cs])
        cw = cw_ref[:, cs]
        gate = cw[0:1] * ge[6:6 + tm] + cw[1:2] * ge[7:7 + tm] + cw[2:3] * ge[8:8 + tm] + cb_ref[:, cs]
        act_ref[:, cs] = (jax.nn.silu(gate) * up).astype(BF16)
    o_ref[...] = x + _rms(_dot(act_ref[...], wd_ref[...]), gpost_ref[...])


def _ffn(x2, gpre, w_in, cw, cb, wd, gpost, layer, seq_len, tm):
    m, d = x2.shape
    f = wd.shape[1]
    const = lambda i: (0, 0)
    resident = lambda shape, col=0: pl.BlockSpec((None,) + shape, lambda i: (layer, 0, col), pipeline_mode=pl.Buffered(1))
    return pl.pallas_call(
        functools.partial(_ffn_body, seq_len=seq_len),
        grid=(m // tm,),
        in_specs=[pl.BlockSpec((tm, d), lambda i: (i, 0)),
                  pl.BlockSpec((8, d), lambda i: (jnp.maximum(i * (tm // 8) - 1, 0), 0)),
                  pl.BlockSpec((1, d), const),
                  resident((d, f), 0), resident((d, f), 1),
                  pl.BlockSpec((None, 3, f), lambda i: (layer, 0, 0)), pl.BlockSpec((1, f), const),
                  resident((f, d)),
                  pl.BlockSpec((1, d), const)],
        out_specs=pl.BlockSpec((tm, d), lambda i: (i, 0)),
        out_shape=jax.ShapeDtypeStruct((m, d), F32),
        scratch_shapes=[pltpu.VMEM((tm + 8, d), BF16), pltpu.VMEM((tm, f), BF16)],
        compiler_params=_params("parallel"),
        name="ffn",
    )(x2, x2, gpre, w_in, w_in, cw, cb, wd, gpost)


def _row_tile(m, pref):
    t = pref
    while m % t:
        t //= 2
    return t


def kernel(x, mem, w_in, b_in, mlstm_norm, sconv_w, rel_bias, hgrn_lb_logits, hgrn_norm, w_mix_out, norm_mix_pre, norm_mix_post, mem_norm, w_cq, w_ck, w_cv, w_co, norm_cross_pre, norm_cross_post, w_ffn_in, ffn_conv_w, ffn_conv_b, w_ffn_out, norm_ffn_pre, norm_ffn_post):
    b, s, d = x.shape
    depth = w_in.shape[0]
    d_ff = w_ffn_out.shape[1]
    assert s % (MOBA_TILES * MOBA_BLOCK) == 0 and s % MLSTM_CHUNK == 0 and s % HGRN_CHUNK == 0
    m = b * s
    tm = _row_tile(m, 512)

    gate0 = 4 * GW
    hq0 = gate0 + 2 * HEADS + 6 * GW
    main_cols = lambda a: jnp.concatenate([a[..., :gate0], a[..., gate0 + 2 * HEADS:hq0 + GW], a[..., hq0 + 2 * GW:]], axis=-1)
    aux_cols = lambda a: jnp.concatenate(
        [a[..., hq0 + GW:hq0 + 2 * GW], a[..., gate0:gate0 + 2 * HEADS],
         jnp.zeros(a.shape[:-1] + (N_AUX - GW - 2 * HEADS,), a.dtype)], axis=-1)
    w_in_b = w_in.astype(BF16)
    wm_all = main_cols(w_in_b)
    wa_all = aux_cols(w_in_b)
    bm_all = main_cols(b_in)[:, None, :]
    ba_all = aux_cols(b_in)[:, None, :]

    w_out = w_mix_out.astype(BF16)
    wq, wk, wv, wo = (w.astype(BF16) for w in (w_cq, w_ck, w_cv, w_co))
    w_ffn = w_ffn_in.astype(BF16)
    w_down = w_ffn_out.astype(BF16)
    row = lambda a, l: a[l][None, :]

    buckets, far_bucket = _bucket_tables()
    bias_tiles = _bias_tiles(rel_bias, buckets)
    assert d_ff % FFN_CHUNK == 0

    x2 = x.reshape(m, d)
    for l in range(depth):
        main, aux, y_c = _inproj(x2, row(norm_mix_pre, l), wm_all, bm_all, wa_all, ba_all, sconv_w, l, s, tm)
        y_c = y_c.reshape(b, s, GW)
        main3 = main.reshape(b, s, N_MAIN)
        aux3 = aux.reshape(b, s, N_AUX)
        y_m = _mlstm(main3, aux3, row(mlstm_norm, l))
        y_a = _moba(main3, rel_bias, bias_tiles, far_bucket)
        y_h = _hgrn(main3, aux3, hgrn_lb_logits, row(hgrn_norm, l), l)
        km, vm = _memkv(mem, row(mem_norm, l), wk, wv, l)
        x2 = _mixout_cross((y_m, y_c, y_a, y_h), w_out, row(norm_mix_post, l), x2.reshape(b, s, d),
                           row(norm_cross_pre, l), wq, km, vm, wo, row(norm_cross_post, l),
                           l, _row_tile(s, 1024)).reshape(m, d)

        x2 = _ffn(x2, row(norm_ffn_pre, l), w_ffn, ffn_conv_w, row(ffn_conv_b, l),
                  w_down, row(norm_ffn_post, l), l, s, _row_tile(m, FFN_ROWS))
    return x2.reshape(b, s, d)
```

```python
import functools
import math

import numpy as np
import jax
import jax.numpy as jnp
from jax import lax
from jax.experimental import pallas as pl
from jax.experimental.pallas import tpu as pltpu

F32 = jnp.float32
BF16 = jnp.bfloat16

HEADS = 4
HEAD_DIM = 64
GW = HEADS * HEAD_DIM
MOBA_BLOCK = 256
MOBA_TOPK = 3
MOBA_TILES = 2
REL_BUCKETS = 32
REL_MAX_DIST = 128
CROSS_HEADS = 4
CROSS_HEAD_DIM = 128
RMS_EPS = 1e-6
NEG_BIG = -1e30
LOG2E = 1.4426950408889634

MLSTM_CHUNK = 256
MLSTM_BATCH = 2
HGRN_CHUNK = 64
HGRN_BATCH = 4
HGRN_SEQ = 1024
FFN_CHUNK = 256
FFN_ROWS = 512
VMEM_LIMIT = 56 * 1024 * 1024

(WB_MQ, WB_MK, WB_MV, WB_MO, WB_CB, WB_CC, WB_CH, WB_AQ, WB_AK, WB_AV, WB_HQ, WB_HI, WB_HG) = range(13)
N_W = 13 * GW
(CB_MQ, CB_MK, CB_MV, CB_MO, CB_AQ, CB_AK, CB_AV, CB_HQ, CB_HI, CB_HG) = range(10)
N_MAIN = 10 * GW
N_AUX = 512


def _rms(x, g):
    return x * lax.rsqrt(jnp.mean(x * x, axis=-1, keepdims=True) + RMS_EPS) * g


def _nt(a, b):
    return lax.dot_general(a, b, (((1,), (1,)), ((), ())), preferred_element_type=F32)


def _tn(a, b):
    return lax.dot_general(a, b, (((0,), (0,)), ((), ())), preferred_element_type=F32)


def _dot(a, b):
    return jnp.dot(a, b, preferred_element_type=F32)


def _head_of(shape, axis):
    return lax.shift_right_logical(lax.broadcasted_iota(jnp.int32, shape, axis), HEAD_DIM.bit_length() - 1)


def _params(*sem):
    return pltpu.CompilerParams(dimension_semantics=sem, vmem_limit_bytes=VMEM_LIMIT)


def _inproj_body(x_ref, g_ref, wm_ref, bm_ref, wa_ref, ba_ref, cw_ref, om_ref, oa_ref, oc_ref, halo_ref, *, seq_len):
    i = pl.program_id(0)
    tm = x_ref.shape[0]

    @pl.when(i == 0)
    def _():
        halo_ref[...] = jnp.zeros_like(halo_ref)

    xn = _rms(x_ref[...], g_ref[...]).astype(BF16)

    def proj(wb):
        return _dot(xn, wm_ref[:, wb * GW:(wb + 1) * GW]) + bm_ref[:, wb * GW:(wb + 1) * GW]

    ob = 0
    for wb in range(N_W // GW):
        if wb not in (WB_CB, WB_CC, WB_CH):
            om_ref[:, ob * GW:(ob + 1) * GW] = proj(wb).astype(om_ref.dtype)
            ob += 1
    oa_ref[...] = _dot(xn, wa_ref[...]) + ba_ref[...]

    u = proj(WB_CC) * proj(WB_CH)
    halo = jnp.where((i * tm) % seq_len == 0, 0.0, halo_ref[...])
    ue = jnp.concatenate([halo, u], axis=0)
    w = cw_ref[...]
    y = proj(WB_CB) * (w[0:1] * ue[6:6 + tm] + w[1:2] * ue[7:7 + tm] + w[2:3] * u)
    oc_ref[...] = y.astype(oc_ref.dtype)
    halo_ref[...] = u[tm - 8:tm]


def _inproj(x2, gain, wm, bm, wa, ba, conv_w, layer, seq_len, tm):
    m, d = x2.shape
    const = lambda i: (0, 0)
    of_layer = lambda a: pl.BlockSpec((None,) + a.shape[1:], lambda i: (layer, 0, 0))
    rows = lambda width: pl.BlockSpec((tm, width), lambda i: (i, 0))
    return pl.pallas_call(
        functools.partial(_inproj_body, seq_len=seq_len),
        grid=(m // tm,),
        in_specs=[rows(d), pl.BlockSpec((1, d), const),
                  of_layer(wm), of_layer(bm), of_layer(wa), of_layer(ba), of_layer(conv_w)],
        out_specs=[rows(N_MAIN), rows(N_AUX), rows(GW)],
        out_shape=[jax.ShapeDtypeStruct((m, N_MAIN), BF16),
                   jax.ShapeDtypeStruct((m, N_AUX), F32),
                   jax.ShapeDtypeStruct((m, GW), BF16)],
        scratch_shapes=[pltpu.VMEM((8, GW), F32)],
        compiler_params=_params("arbitrary"),
        name="inproj",
    )(x2, gain, wm, bm, wa, ba, conv_w)


def _scan_lanes(x, op, seg):
    pos = lax.broadcasted_iota(jnp.int32, x.shape, 1) & (seg - 1)
    s = 1
    while s < seg:
        x = jnp.where(pos >= s, op(x, pltpu.roll(x, s, 1)), x)
        s *= 2
    return x


def _head_rms_gate(hval, gain, gate, bones):
    sq = hval * hval
    hi = sq.astype(BF16)
    lo = (sq - hi.astype(F32)).astype(BF16)
    ms = (_dot(hi, bones) + _dot(lo, bones)) * (1.0 / HEAD_DIM)
    return hval * lax.rsqrt(ms + RMS_EPS) * gain * gate


def _block_ones():
    r = _head_of((GW, GW), 0)
    c = _head_of((GW, GW), 1)
    return r == c


def _split3(x):
    hi = x.astype(BF16).astype(F32)
    r = x - hi
    mid = r.astype(BF16).astype(F32)
    lo = (r - mid).astype(BF16).astype(F32)
    return hi, mid, lo


def _mlstm_body(q_ref, k_ref, v_ref, o_ref, g_ref, gain_ref, y_ref, ct_ref, n_ref, m_ref, g_ref2, c_ref2, cm_ref2):
    s_len = q_ref.shape[1]
    L = MLSTM_CHUNK
    iota = lambda shape, axis: lax.broadcasted_iota(jnp.int32, shape, axis)
    bd = _block_ones()
    bones = jnp.where(bd, 1.0, 0.0).astype(BF16)
    hid_l = _head_of((L, GW), 1)
    hmask = [hid_l == h for h in range(HEADS)]
    tril = iota((L, L), 0) >= iota((L, L), 1)
    row8 = iota((8, L), 0)
    row16 = iota((16, L), 0)
    hid1 = _head_of((1, GW), 1)
    gain = gain_ref[...]
    ones_bd = jnp.where(_head_of((HEADS * L, GW), 1) == iota((HEADS * L, GW), 0) // L, 1.0, 0.0).astype(BF16)
    csel = iota((128, 3 * GW), 1)
    esel = jnp.where(iota((128, 3 * GW), 0) == 16 + 4 * (csel // GW) + (csel % GW) // HEAD_DIM, 1.0, 0.0).astype(BF16)
    ones8 = jnp.ones((8, L), BF16)

    ct_ref[...] = jnp.zeros_like(ct_ref)
    n_ref[...] = jnp.zeros_like(n_ref)
    m_ref[...] = jnp.zeros_like(m_ref)

    for bb in range(q_ref.shape[0]):
        gt = jnp.concatenate([jnp.transpose(g_ref[bb, c * L:(c + 1) * L, :])[0:8] for c in range(s_len // L)], axis=1)
        g = _scan_lanes(jax.nn.log_sigmoid(pltpu.roll(gt, 4, 0)), jnp.add, L)
        cc = gt - g
        g_ref2[bb] = g
        c_ref2[bb] = cc
        cm_ref2[bb] = _scan_lanes(cc, jnp.maximum, L)

    def prep(bb, r0):
        q = q_ref[bb, pl.ds(r0, L), :]
        k = (k_ref[bb, pl.ds(r0, L), :].astype(F32) * (HEAD_DIM ** -0.5)).astype(BF16)
        v = v_ref[bb, pl.ds(r0, L), :]
        g = g_ref2[bb, :, pl.ds(r0, L)]
        cc = c_ref2[bb, :, pl.ds(r0, L)]
        mprev = m_ref[bb][:, 0:1]
        pp = jnp.maximum(mprev, cm_ref2[bb, :, pl.ds(r0, L)])
        ein = jnp.exp(mprev - pp)
        p_last = pp[:, L - 1:L]
        ws = jnp.exp(cc - p_last)
        decay = jnp.exp(mprev - p_last)
        m_new = g[:, L - 1:L] + p_last
        emt = jnp.exp(-(g + pp))
        p_hi, p_mid, p_lo = _split3(pp)
        b0 = jnp.where(row8 < 4, p_hi, pltpu.roll(p_mid, 4, 0))
        b1 = jnp.where(row8 < 4, p_lo, jnp.where(row8 < 7, 1.0, 0.0))
        b2 = jnp.where(row8 < 4, ein, pltpu.roll(ws, 4, 0))
        b3 = jnp.where(row8 < 4, emt, 0.0)
        cols = jnp.transpose(jnp.concatenate([b0, b1, b2, b3, jnp.zeros((96, L), F32)], axis=0)).astype(BF16)
        return dict(bb=bb, r0=r0, q=q, k=k, v=v, cpieces=_split3(cc), cols=cols, decay=decay, m_new=m_new, w=[])

    def head_weights(st, h):
        c_hi, c_mid, c_lo = st["cpieces"]
        pick = (row16 == h) | (row16 == 4 + h) | (row16 == 8 + h)
        bh = jnp.where(pick, -1.0, jnp.where(row16 == 12, c_hi[h:h + 1], jnp.where(
            row16 == 13, c_mid[h:h + 1], jnp.where(row16 == 14, c_lo[h:h + 1], 0.0))))
        bh = jnp.concatenate([bh, jnp.zeros((112, L), F32)], axis=0).astype(BF16)
        expo = _dot(st["cols"], bh)
        sc = _nt(jnp.where(hmask[h], st["q"], jnp.zeros_like(st["q"])), st["k"])
        st["w"].append((jnp.exp(jnp.where(tril, expo, NEG_BIG)) * sc).astype(BF16))

    def finish(st):
        bb, r0, q, k, v, cols = st["bb"], st["r0"], st["q"], st["k"], st["v"], st["cols"]
        w_cat = jnp.concatenate(st["w"], axis=1)
        v_bd = jnp.concatenate([jnp.where(hmask[h], v, jnp.zeros_like(v)) for h in range(HEADS)], axis=0)
        num = _dot(w_cat, v_bd)
        den = _dot(w_cat, ones_bd)
        x3 = _dot(cols, esel)
        ein_x, ws_x, emt_x = x3[:, :GW], x3[:, GW:2 * GW], x3[:, 2 * GW:]

        ct = ct_ref[bb]
        nrow = n_ref[bb][0:1]
        nbd = jnp.where(bd, jnp.broadcast_to(nrow, (GW, GW)), 0.0).astype(BF16)
        num = num + ein_x * _nt(q, ct.astype(BF16))
        den = den + ein_x * _nt(q, nbd)
        hval = num / jnp.maximum(jnp.abs(den), emt_x)
        gate = jax.nn.sigmoid(o_ref[bb, pl.ds(r0, L), :].astype(F32))
        y_ref[bb, pl.ds(r0, L), :] = _head_rms_gate(hval, gain, gate, bones).astype(y_ref.dtype)

        decay = st["decay"]
        dec_l = jnp.where(hid1 == 0, decay[0:1], jnp.where(hid1 == 1, decay[1:2], jnp.where(hid1 == 2, decay[2:3], decay[3:4])))
        vw = (v.astype(F32) * ws_x).astype(BF16)
        kw = (k.astype(F32) * ws_x).astype(BF16)
        ct_ref[bb] = ct * dec_l + jnp.where(bd, _tn(vw, k), 0.0)
        n_ref[bb] = jnp.broadcast_to(nrow * dec_l + _dot(ones8, kw)[0:1], n_ref.shape[1:])
        m_ref[bb] = jnp.broadcast_to(st["m_new"], m_ref.shape[1:])

    def step(c, carry):
        r0 = pl.multiple_of(c * L, L)
        states = [prep(bb, r0) for bb in range(q_ref.shape[0])]
        for h in range(HEADS):
            for st in states:
                head_weights(st, h)
        for st in states:
            finish(st)
        return carry

    lax.fori_loop(0, s_len // L, step, 0)


def _mlstm(main3, aux3, gain):
    b, s, _ = main3.shape
    nb = MLSTM_BATCH if b % MLSTM_BATCH == 0 else 1
    col = lambda cb: pl.BlockSpec((nb, s, GW), lambda i: (i, 0, cb))
    return pl.pallas_call(
        _mlstm_body,
        grid=(b // nb,),
        in_specs=[col(CB_MQ), col(CB_MK), col(CB_MV), col(CB_MO),
                  pl.BlockSpec((nb, s, 128), lambda i: (i, 0, 2)),
                  pl.BlockSpec((1, GW), lambda i: (0, 0))],
        out_specs=pl.BlockSpec((nb, s, GW), lambda i: (i, 0, 0)),
        out_shape=jax.ShapeDtypeStruct((b, s, GW), BF16),
        scratch_shapes=[pltpu.VMEM((nb, GW, GW), F32), pltpu.VMEM((nb, 8, GW), F32), pltpu.VMEM((nb, 8, 128), F32)]
        + [pltpu.VMEM((nb, 8, s), F32)] * 3,
        compiler_params=_params("parallel"),
        name="mlstm",
    )(main3, main3, main3, main3, aux3, gain)


def _hgrn_body(lbl_ref, q_ref, i_ref, g_ref, f_ref, gain_ref, y_ref, st_ref, *, layer):
    nbat, s_len = q_ref.shape[0], q_ref.shape[1]
    C = HGRN_CHUNK
    R = nbat * C
    bd = _block_ones()
    bones = jnp.where(bd, 1.0, 0.0).astype(BF16)
    hid_r = _head_of((R, GW), 1)
    hm_b = [jnp.where(hid_r == h, 1.0, 0.0).astype(BF16) for h in range(HEADS)]
    gain = gain_ref[...]

    lg = lbl_ref[...]
    ex = jnp.exp(lg - jnp.max(lg, axis=0, keepdims=True))
    soft = ex / jnp.sum(ex, axis=0, keepdims=True)
    lb = jnp.zeros((1, GW), F32)
    for j in range(1, layer + 1):
        lb = lb + soft[j:j + 1]

    tq = lax.broadcasted_iota(jnp.int32, (C, HEADS * C), 0)
    tk = lax.broadcasted_iota(jnp.int32, (C, HEADS * C), 1) & (C - 1)
    level_mask = {}
    bsz = C
    while bsz >= 2:
        half = bsz // 2
        same = (tq // bsz) == (tk // bsz)
        level_mask[bsz] = same & ((tq % bsz) >= half) & ((tk % bsz) < half)
        bsz = half
    eye = tq == tk
    rowpos = lax.broadcasted_iota(jnp.int32, (R, GW), 0) & (C - 1)
    steps = [1 << j for j in range(C.bit_length() - 1)]
    past_start = {sh: rowpos >= sh for sh in steps}
    upper_half = {sh: (rowpos & sh) != 0 for sh in steps}

    @pl.when(pl.program_id(1) == 0)
    def _():
        st_ref[...] = jnp.zeros_like(st_ref)

    def load(ref, r0):
        return ref[:, pl.ds(r0, C), :].reshape(R, GW)

    def stack_heads(a, bb):
        return jnp.concatenate([(a * hm_b[h])[bb * C:(bb + 1) * C] for h in range(HEADS)], axis=0)

    def step(c, carry):
        r0 = pl.multiple_of(c * C, C)
        f = lb + (1.0 - lb) * jax.nn.sigmoid(load(f_ref, r0))
        lf = jnp.log2(f)
        kk = 1.0 - f
        qq = jax.nn.silu(load(q_ref, r0).astype(F32))
        vv = load(i_ref, r0)
        rows = [slice(bb * C, (bb + 1) * C) for bb in range(nbat)]

        gcum = lf
        for sh in steps:
            gcum = gcum + jnp.where(past_start[sh], pltpu.roll(gcum, sh, 0), 0.0)
        st = [st_ref[bb] for bb in range(nbat)]
        qin = (qq * jnp.exp2(gcum)).astype(BF16)
        out = [_nt(qin[rows[bb]], st[bb].astype(BF16)) for bb in range(nbat)]

        qb, kb = qq.astype(BF16), kk.astype(BF16)
        att = [jnp.where(eye, _nt(qb[rows[bb]], stack_heads(kb, bb)), 0.0) for bb in range(nbat)]
        ge = gcum
        for half in steps:
            qd = (qq * jnp.exp2(jnp.minimum(gcum - pltpu.roll(ge, half, 0), 0.0))).astype(BF16)
            kd = (kk * jnp.exp2(ge - gcum)).astype(BF16)
            att = [jnp.where(level_mask[2 * half], _nt(qd[rows[bb]], stack_heads(kd, bb)), att[bb]) for bb in range(nbat)]
            ge = jnp.where(upper_half[half], ge, pltpu.roll(ge, R - half, 0))
        for bb in range(nbat):
            out[bb] = out[bb] + _dot(att[bb].astype(BF16), stack_heads(vv, bb))

        gate = jax.nn.silu(load(g_ref, r0).astype(F32))
        y = _head_rms_gate(jnp.concatenate(out, axis=0), gain, gate, bones).astype(y_ref.dtype)
        y_ref[:, pl.ds(r0, C), :] = y.reshape(nbat, C, GW)

        for bb in range(nbat):
            g_last = gcum[bb * C + C - 1:bb * C + C, :]
            kdec = (kk[rows[bb]] * jnp.exp2(g_last - gcum[rows[bb]])).astype(BF16)
            st_ref[bb] = st[bb] * jnp.exp2(g_last) + jnp.where(bd, _tn(vv[rows[bb]], kdec), 0.0)
        return carry

    lax.fori_loop(0, s_len // C, step, 0)


def _hgrn(main3, aux3, lb_logits, gain, layer):
    b, s, _ = main3.shape
    nbat = HGRN_BATCH if b % HGRN_BATCH == 0 else 1
    sblk = HGRN_SEQ if s % HGRN_SEQ == 0 else s
    col = lambda cb: pl.BlockSpec((nbat, sblk, GW), lambda i, j: (i, j, cb))
    return pl.pallas_call(
        functools.partial(_hgrn_body, layer=layer),
        grid=(b // nbat, s // sblk),
        in_specs=[pl.BlockSpec(lb_logits.shape, lambda i, j: (0, 0)),
                  col(CB_HQ), col(CB_HI), col(CB_HG),
                  pl.BlockSpec((nbat, sblk, GW), lambda i, j: (i, j, 0)),
                  pl.BlockSpec((1, GW), lambda i, j: (0, 0))],
        out_specs=pl.BlockSpec((nbat, sblk, GW), lambda i, j: (i, j, 0)),
        out_shape=jax.ShapeDtypeStruct((b, s, GW), BF16),
        scratch_shapes=[pltpu.VMEM((nbat, GW, GW), F32)],
        compiler_params=_params("parallel", "arbitrary"),
        name="hgrn2",
    )(lb_logits, main3, main3, main3, aux3, gain)


def _bucket_tables():
    def bucket(dist):
        n = np.maximum(dist, 0)
        exact = REL_BUCKETS // 2
        nf = np.maximum(n, 1).astype(np.float32)
        large = exact + (np.log(nf / np.float32(exact)) / np.float32(math.log(REL_MAX_DIST / exact))
                         * np.float32(REL_BUCKETS - exact)).astype(np.int32)
        large = np.minimum(large, REL_BUCKETS - 1)
        return np.where(n < exact, n, large).astype(np.int32)
    tk = np.arange(MOBA_BLOCK)[:, None]
    tq = np.arange(MOBA_BLOCK)[None, :]
    own = np.where(tq - tk >= 0, bucket(tq - tk), REL_BUCKETS)
    adj = bucket(MOBA_BLOCK + tq - tk)
    far = int(bucket(np.array([2 * MOBA_BLOCK]))[0])
    return np.stack([own, adj]).astype(np.int32), far


def _bias_body(rb_ref, bk_ref, o_ref):
    bk = bk_ref[...]
    bs = bk.shape[-1]
    for h in range(HEADS):
        acc = jnp.full(bk.shape, NEG_BIG, F32)
        for j in range(REL_BUCKETS):
            acc = jnp.where(bk == j, rb_ref[j, h] * LOG2E, acc)
        o_ref[:, :, h * bs:(h + 1) * bs] = acc


def _bias_tiles(rel_bias, buckets):
    two, bk, bq = buckets.shape
    return pl.pallas_call(
        _bias_body,
        in_specs=[pl.BlockSpec(memory_space=pltpu.SMEM), pl.BlockSpec(buckets.shape, lambda: (0, 0, 0))],
        out_specs=pl.BlockSpec((two, bk, HEADS * bq), lambda: (0, 0, 0)),
        out_shape=jax.ShapeDtypeStruct((two, bk, HEADS * bq), F32),
        name="moba_bias",
    )(rel_bias, jnp.asarray(buckets))


VT_ROWS = HEAD_DIM + 16


def _moba_body(rb_ref, q_ref, k_ref, v_ref, bias_ref, y_ref, vt_ref, km_ref, qbd_ref, *, far_bucket):
    jp = pl.program_id(1)
    BS = MOBA_BLOCK
    nb = k_ref.shape[1] // BS
    n_sel = min(MOBA_TOPK, nb - 1)
    scale = HEAD_DIM ** -0.5

    @pl.when(jp == 0)
    def _():
        hid = _head_of((1, GW), 1)
        vt_ref[:, HEAD_DIM:, :] = jnp.ones((HEADS, VT_ROWS - HEAD_DIM, nb * BS), BF16)
        for n in range(nb):
            cols = slice(n * BS, (n + 1) * BS)
            vt = jnp.transpose(v_ref[0, cols, :].astype(F32)).astype(BF16)
            km = jnp.mean(k_ref[0, cols, :].astype(F32), axis=0, keepdims=True)
            for h in range(HEADS):
                vt_ref[h, 0:HEAD_DIM, cols] = vt[h * HEAD_DIM:(h + 1) * HEAD_DIM]
                km_ref[h * nb + n:h * nb + n + 1, :] = jnp.where(hid == h, km, 0.0).astype(BF16)

    hid_q = _head_of((BS, GW), 1)
    blk = lax.broadcasted_iota(jnp.int32, (nb, BS), 0)
    sels = []
    for t in range(MOBA_TILES):
        i = MOBA_TILES * jp + t
        q = q_ref[0, t * BS:(t + 1) * BS, :]
        qs = (q.astype(F32) * (scale * LOG2E)).astype(BF16)
        for h in range(HEADS):
            qbd_ref[t, h * BS:(h + 1) * BS, :] = jnp.where(hid_q == h, qs, jnp.zeros_like(qs))
        gate = _nt(km_ref[...], q)
        sel = []
        for h in range(HEADS):
            g = jnp.where(blk < i, gate[h * nb:(h + 1) * nb], NEG_BIG)
            rank = jnp.zeros((nb, BS), jnp.int32)
            for mrow in range(nb):
                gm = g[mrow:mrow + 1]
                ahead = (gm > g) | ((gm == g) & (mrow < blk))
                rank = rank + jnp.where(ahead, 1, 0)
            chosen = (rank < n_sel) & (blk < i)
            far = jnp.where(blk < i - 1, rb_ref[far_bucket, h] * LOG2E, 0.0)
            sel.append(jnp.where(chosen, far, NEG_BIG))
        sels.append(jnp.concatenate(sel, axis=1))

    def attend(pair):
        lanes = [(t, h) for t in range(MOBA_TILES) for h in range(HEADS)]
        past = [MOBA_TILES * pair + t for t in range(MOBA_TILES)]
        s_all = {(t, h): _nt(k_ref[0, 0:(past[t] + 1) * BS, :], qbd_ref[t, h * BS:(h + 1) * BS, :]) for t, h in lanes}
        m = dict.fromkeys(lanes)
        acc = dict.fromkeys(lanes)
        for n in range(max(past) + 1):
            live = [(t, h) for t, h in lanes if n <= past[t]]
            p, alpha = {}, {}
            for t, h in live:
                hs = slice(h * BS, (h + 1) * BS)
                s = s_all[t, h][n * BS:(n + 1) * BS]
                if n == past[t]:
                    s, r = s + bias_ref[0, :, hs], None
                elif n == past[t] - 1:
                    s, r = s + bias_ref[1, :, hs], sels[t][n:n + 1, hs]
                else:
                    r = sels[t][n:n + 1, hs]
                bm = jnp.max(s, axis=0, keepdims=True)
                bm = bm if r is None else bm + r
                m_new = bm if m[t, h] is None else jnp.maximum(m[t, h], bm)
                p[t, h] = jnp.exp2(s - (m_new if r is None else m_new - r)).astype(BF16)
                alpha[t, h] = None if m[t, h] is None else jnp.exp2(m[t, h] - m_new)
                m[t, h] = m_new
            for t, h in live:
                pv = _dot(vt_ref[h, :, n * BS:(n + 1) * BS], p[t, h])
                acc[t, h] = pv if alpha[t, h] is None else alpha[t, h] * acc[t, h] + pv
        for t in range(MOBA_TILES):
            outs = [acc[t, h][0:HEAD_DIM] * (1.0 / acc[t, h][HEAD_DIM:HEAD_DIM + 1]) for h in range(HEADS)]
            y_ref[0, t * BS:(t + 1) * BS, :] = jnp.transpose(jnp.concatenate(outs, axis=0)).astype(y_ref.dtype)

    for pair in range(nb // MOBA_TILES):
        pl.when(jp == pair)(functools.partial(attend, pair))


def _moba(main3, rel_bias, bias_tiles, far_bucket):
    b, s, _ = main3.shape
    nb = s // MOBA_BLOCK
    rows = MOBA_TILES * MOBA_BLOCK
    return pl.pallas_call(
        functools.partial(_moba_body, far_bucket=far_bucket),
        grid=(b, nb // MOBA_TILES),
        in_specs=[pl.BlockSpec(memory_space=pltpu.SMEM),
                  pl.BlockSpec((1, rows, GW), lambda bi, i: (bi, i, CB_AQ)),
                  pl.BlockSpec((1, s, GW), lambda bi, i: (bi, 0, CB_AK)),
                  pl.BlockSpec((1, s, GW), lambda bi, i: (bi, 0, CB_AV)),
                  pl.BlockSpec(bias_tiles.shape, lambda bi, i: (0, 0, 0), pipeline_mode=pl.Buffered(1))],
        out_specs=pl.BlockSpec((1, rows, GW), lambda bi, i: (bi, i, 0)),
        out_shape=jax.ShapeDtypeStruct((b, s, GW), BF16),
        scratch_shapes=[pltpu.VMEM((HEADS, VT_ROWS, s), BF16),
                        pltpu.VMEM((HEADS * nb, GW), BF16),
                        pltpu.VMEM((MOBA_TILES, HEADS * MOBA_BLOCK, GW), BF16)],
        compiler_params=_params("parallel", "arbitrary"),
        name="moba",
    )(rel_bias, main3, main3, main3, bias_tiles)


def _memkv_body(mem_ref, g_ref, wk_ref, wv_ref, k_ref, v_ref):
    mn = _rms(mem_ref[0], g_ref[...]).astype(BF16)
    k_ref[0] = _dot(mn, wk_ref[...]).astype(BF16)
    v_ref[0] = _dot(mn, wv_ref[...]).astype(BF16)


def _memkv(mem, gain, wk, wv, layer):
    b, nm, d = mem.shape
    cw = wk.shape[-1]
    const = lambda i: (0, 0)
    of_layer = lambda a: pl.BlockSpec((None,) + a.shape[1:], lambda i: (layer, 0, 0))
    return pl.pallas_call(
        _memkv_body,
        grid=(b,),
        in_specs=[pl.BlockSpec((1, nm, d), lambda i: (i, 0, 0)), pl.BlockSpec((1, d), const),
                  of_layer(wk), of_layer(wv)],
        out_specs=[pl.BlockSpec((1, nm, cw), lambda i: (i, 0, 0))] * 2,
        out_shape=[jax.ShapeDtypeStruct((b, nm, cw), BF16)] * 2,
        compiler_params=_params("parallel"),
        name="memkv",
    )(mem, gain, wk, wv)


def _mixout_cross_body(ym_ref, yc_ref, ya_ref, yh_ref, wout_ref, gmix_ref, x_ref,
                       gpre_ref, wq_ref, k_ref, v_ref, wo_ref, gpost_ref, o_ref):
    h = (_dot(ym_ref[0], wout_ref[0 * GW:1 * GW, :]) + _dot(yc_ref[0], wout_ref[1 * GW:2 * GW, :])
         + _dot(ya_ref[0], wout_ref[2 * GW:3 * GW, :]) + _dot(yh_ref[0], wout_ref[3 * GW:4 * GW, :]))
    x = x_ref[0] + _rms(h, gmix_ref[...])
    xn = _rms(x, gpre_ref[...]).astype(BF16)
    q = _dot(xn, wq_ref[...]).astype(BF16)
    k = k_ref[0]
    v = v_ref[0]
    outs = []
    for hh in range(CROSS_HEADS):
        sl = slice(hh * CROSS_HEAD_DIM, (hh + 1) * CROSS_HEAD_DIM)
        s = _nt(q[:, sl], k[:, sl]) * (CROSS_HEAD_DIM ** -0.5)
        p = jnp.exp(s - jnp.max(s, axis=-1, keepdims=True))
        inv = 1.0 / jnp.sum(p, axis=-1, keepdims=True)
        outs.append((_dot(p.astype(BF16), v[:, sl]) * inv).astype(BF16))
    o = jnp.concatenate(outs, axis=-1)
    o_ref[0] = x + _rms(_dot(o, wo_ref[...]), gpost_ref[...])


def _mixout_cross(ys, wout, gmix, x3, gpre, wq, km, vm, wo, gpost, layer, tq):
    b, s, d = x3.shape
    nm, cw = km.shape[1:]
    const = lambda bi, i: (0, 0)
    of_layer = lambda a: pl.BlockSpec((None,) + a.shape[1:], lambda bi, i: (layer, 0, 0))
    tile = lambda width: pl.BlockSpec((1, tq, width), lambda bi, i: (bi, i, 0))
    mem = pl.BlockSpec((1, nm, cw), lambda bi, i: (bi, 0, 0))
    return pl.pallas_call(
        _mixout_cross_body,
        grid=(b, s // tq),
        in_specs=[tile(GW), tile(GW), tile(GW), tile(GW),
                  of_layer(wout), pl.BlockSpec((1, d), const), tile(d),
                  pl.BlockSpec((1, d), const), of_layer(wq), mem, mem,
                  of_layer(wo), pl.BlockSpec((1, d), const)],
        out_specs=tile(d),
        out_shape=jax.ShapeDtypeStruct((b, s, d), F32),
        compiler_params=_params("parallel", "parallel"),
        name="mixout_cross",
    )(*ys, wout, gmix, x3, gpre, wq, km, vm, wo, gpost)


def _ffn_body(x_ref, xh_ref, gpre_ref, wg_ref, wu_ref, cw_ref, cb_ref, wd_ref, gpost_ref, o_ref,
              xn_ref, act_ref, *, seq_len):
    i = pl.program_id(0)
    tm = x_ref.shape[0]
    g = gpre_ref[...]
    x = x_ref[...]
    xn_ref[8:, :] = _rms(x, g).astype(BF16)
    starts_seq = (i * tm) % seq_len == 0
    xn_ref[0:8, :] = jnp.where(starts_seq, 0.0, _rms(xh_ref[...], g)).astype(BF16)
    xe = xn_ref[...]
    for c in range(0, act_ref.shape[1], FFN_CHUNK):
        cs = slice(c, c + FFN_CHUNK)
        ge = _dot(xe, wg_ref[:, cs])
        up = _dot(xe[8:], wu_ref[:, cs])
        cw = cw_ref[:, cs]
        gate = cw[0:1] * ge[6:6 + tm] + cw[1:2] * ge[7:7 + tm] + cw[2:3] * ge[8:8 + tm] + cb_ref[:, cs]
        act_ref[:, cs] = (jax.nn.silu(gate) * up).astype(BF16)
    o_ref[...] = x + _rms(_dot(act_ref[...], wd_ref[...]), gpost_ref[...])


def _ffn(x2, gpre, w_in, cw, cb, wd, gpost, layer, seq_len, tm):
    m, d = x2.shape
    f = wd.shape[1]
    const = lambda i: (0, 0)
    resident = lambda shape, col=0: pl.BlockSpec((None,) + shape, lambda i: (layer, 0, col), pipeline_mode=pl.Buffered(1))
    return pl.pallas_call(
        functools.partial(_ffn_body, seq_len=seq_len),
        grid=(m // tm,),
        in_specs=[pl.BlockSpec((tm, d), lambda i: (i, 0)),
                  pl.BlockSpec((8, d), lambda i: (jnp.maximum(i * (tm // 8) - 1, 0), 0)),
                  pl.BlockSpec((1, d), const),
                  resident((d, f), 0), resident((d, f), 1),
                  pl.BlockSpec((None, 3, f), lambda i: (layer, 0, 0)), pl.BlockSpec((1, f), const),
                  resident((f, d)),
                  pl.BlockSpec((1, d), const)],
        out_specs=pl.BlockSpec((tm, d), lambda i: (i, 0)),
        out_shape=jax.ShapeDtypeStruct((m, d), F32),
        scratch_shapes=[pltpu.VMEM((tm + 8, d), BF16), pltpu.VMEM((tm, f), BF16)],
        compiler_params=_params("parallel"),
        name="ffn",
    )(x2, x2, gpre, w_in, w_in, cw, cb, wd, gpost)


def _row_tile(m, pref):
    t = pref
    while m % t:
        t //= 2
    return t


def kernel(x, mem, w_in, b_in, mlstm_norm, sconv_w, rel_bias, hgrn_lb_logits, hgrn_norm, w_mix_out, norm_mix_pre, norm_mix_post, mem_norm, w_cq, w_ck, w_cv, w_co, norm_cross_pre, norm_cross_post, w_ffn_in, ffn_conv_w, ffn_conv_b, w_ffn_out, norm_ffn_pre, norm_ffn_post):
    b, s, d = x.shape
    depth = w_in.shape[0]
    d_ff = w_ffn_out.shape[1]
    assert s % (MOBA_TILES * MOBA_BLOCK) == 0 and s % MLSTM_CHUNK == 0 and s % HGRN_CHUNK == 0
    m = b * s
    tm = _row_tile(m, 512)

    gate0 = 4 * GW
    hq0 = gate0 + 2 * HEADS + 6 * GW
    main_cols = lambda a: jnp.concatenate([a[..., :gate0], a[..., gate0 + 2 * HEADS:hq0 + GW], a[..., hq0 + 2 * GW:]], axis=-1)
    aux_cols = lambda a: jnp.concatenate(
        [a[..., hq0 + GW:hq0 + 2 * GW], a[..., gate0:gate0 + 2 * HEADS],
         jnp.zeros(a.shape[:-1] + (N_AUX - GW - 2 * HEADS,), a.dtype)], axis=-1)
    w_in_b = w_in.astype(BF16)
    wm_all = main_cols(w_in_b)
    wa_all = aux_cols(w_in_b)
    bm_all = main_cols(b_in)[:, None, :]
    ba_all = aux_cols(b_in)[:, None, :]

    w_out = w_mix_out.astype(BF16)
    wq, wk, wv, wo = (w.astype(BF16) for w in (w_cq, w_ck, w_cv, w_co))
    w_ffn = w_ffn_in.astype(BF16)
    w_down = w_ffn_out.astype(BF16)
    row = lambda a, l: a[l][None, :]

    buckets, far_bucket = _bucket_tables()
    bias_tiles = _bias_tiles(rel_bias, buckets)
    assert d_ff % FFN_CHUNK == 0

    x2 = x.reshape(m, d)
    for l in range(depth):
        main, aux, y_c = _inproj(x2, row(norm_mix_pre, l), wm_all, bm_all, wa_all, ba_all, sconv_w, l, s, tm)
        y_c = y_c.reshape(b, s, GW)
        main3 = main.reshape(b, s, N_MAIN)
        aux3 = aux.reshape(b, s, N_AUX)
        y_m = _mlstm(main3, aux3, row(mlstm_norm, l))
        y_a = _moba(main3, rel_bias, bias_tiles, far_bucket)
        y_h = _hgrn(main3, aux3, hgrn_lb_logits, row(hgrn_norm, l), l)
        km, vm = _memkv(mem, row(mem_norm, l), wk, wv, l)
        x2 = _mixout_cross((y_m, y_c, y_a, y_h), w_out, row(norm_mix_post, l), x2.reshape(b, s, d),
                           row(norm_cross_pre, l), wq, km, vm, wo, row(norm_cross_post, l),
                           l, _row_tile(s, 1024)).reshape(m, d)

        x2 = _ffn(x2, row(norm_ffn_pre, l), w_ffn, ffn_conv_w, row(ffn_conv_b, l),
                  w_down, row(norm_ffn_post, l), l, s, _row_tile(m, FFN_ROWS))
    return x2.reshape(b, s, d)
```

```python
import functools
import math

import numpy as np
import jax
import jax.numpy as jnp
from jax import lax
from jax.experimental import pallas as pl
from jax.experimental.pallas import tpu as pltpu

F32 = jnp.float32
BF16 = jnp.bfloat16

HEADS = 4
HEAD_DIM = 64
GW = HEADS * HEAD_DIM
MOBA_BLOCK = 256
MOBA_TOPK = 3
MOBA_TILES = 2
REL_BUCKETS = 32
REL_MAX_DIST = 128
CROSS_HEADS = 4
CROSS_HEAD_DIM = 128
RMS_EPS = 1e-6
NEG_BIG = -1e30
LOG2E = 1.4426950408889634

MLSTM_CHUNK = 256
MLSTM_BATCH = 2
HGRN_CHUNK = 64
HGRN_BATCH = 4
HGRN_SEQ = 1024
FFN_CHUNK = 256
FFN_ROWS = 512
VMEM_LIMIT = 56 * 1024 * 1024

(WB_MQ, WB_MK, WB_MV, WB_MO, WB_CB, WB_CC, WB_CH, WB_AQ, WB_AK, WB_AV, WB_HQ, WB_HI, WB_HG) = range(13)
N_W = 13 * GW
(CB_MQ, CB_MK, CB_MV, CB_MO, CB_AQ, CB_AK, CB_AV, CB_HQ, CB_HI, CB_HG) = range(10)
N_MAIN = 10 * GW
N_AUX = 512


def _rms(x, g):
    return x * lax.rsqrt(jnp.mean(x * x, axis=-1, keepdims=True) + RMS_EPS) * g


def _nt(a, b):
    return lax.dot_general(a, b, (((1,), (1,)), ((), ())), preferred_element_type=F32)


def _tn(a, b):
    return lax.dot_general(a, b, (((0,), (0,)), ((), ())), preferred_element_type=F32)


def _dot(a, b):
    return jnp.dot(a, b, preferred_element_type=F32)


def _head_of(shape, axis):
    return lax.shift_right_logical(lax.broadcasted_iota(jnp.int32, shape, axis), HEAD_DIM.bit_length() - 1)


def _params(*sem):
    return pltpu.CompilerParams(dimension_semantics=sem, vmem_limit_bytes=VMEM_LIMIT)


def _inproj_body(x_ref, g_ref, wm_ref, bm_ref, wa_ref, ba_ref, cw_ref, om_ref, oa_ref, oc_ref, halo_ref, *, seq_len):
    i = pl.program_id(0)
    tm = x_ref.shape[0]

    @pl.when(i == 0)
    def _():
        halo_ref[...] = jnp.zeros_like(halo_ref)

    xn = _rms(x_ref[...], g_ref[...]).astype(BF16)

    def proj(wb):
        return _dot(xn, wm_ref[:, wb * GW:(wb + 1) * GW]) + bm_ref[:, wb * GW:(wb + 1) * GW]

    ob = 0
    for wb in range(N_W // GW):
        if wb not in (WB_CB, WB_CC, WB_CH):
            om_ref[:, ob * GW:(ob + 1) * GW] = proj(wb).astype(om_ref.dtype)
            ob += 1
    oa_ref[...] = _dot(xn, wa_ref[...]) + ba_ref[...]

    u = proj(WB_CC) * proj(WB_CH)
    halo = jnp.where((i * tm) % seq_len == 0, 0.0, halo_ref[...])
    ue = jnp.concatenate([halo, u], axis=0)
    w = cw_ref[...]
    y = proj(WB_CB) * (w[0:1] * ue[6:6 + tm] + w[1:2] * ue[7:7 + tm] + w[2:3] * u)
    oc_ref[...] = y.astype(oc_ref.dtype)
    halo_ref[...] = u[tm - 8:tm]


def _inproj(x2, gain, wm, bm, wa, ba, conv_w, layer, seq_len, tm):
    m, d = x2.shape
    const = lambda i: (0, 0)
    of_layer = lambda a: pl.BlockSpec((None,) + a.shape[1:], lambda i: (layer, 0, 0))
    rows = lambda width: pl.BlockSpec((tm, width), lambda i: (i, 0))
    return pl.pallas_call(
        functools.partial(_inproj_body, seq_len=seq_len),
        grid=(m // tm,),
        in_specs=[rows(d), pl.BlockSpec((1, d), const),
                  of_layer(wm), of_layer(bm), of_layer(wa), of_layer(ba), of_layer(conv_w)],
        out_specs=[rows(N_MAIN), rows(N_AUX), rows(GW)],
        out_shape=[jax.ShapeDtypeStruct((m, N_MAIN), BF16),
                   jax.ShapeDtypeStruct((m, N_AUX), F32),
                   jax.ShapeDtypeStruct((m, GW), BF16)],
        scratch_shapes=[pltpu.VMEM((8, GW), F32)],
        compiler_params=_params("arbitrary"),
        name="inproj",
    )(x2, gain, wm, bm, wa, ba, conv_w)


def _scan_lanes(x, op, seg):
    pos = lax.broadcasted_iota(jnp.int32, x.shape, 1) & (seg - 1)
    s = 1
    while s < seg:
        x = jnp.where(pos >= s, op(x, pltpu.roll(x, s, 1)), x)
        s *= 2
    return x


def _head_rms_gate(hval, gain, gate, bones):
    sq = hval * hval
    hi = sq.astype(BF16)
    lo = (sq - hi.astype(F32)).astype(BF16)
    ms = (_dot(hi, bones) + _dot(lo, bones)) * (1.0 / HEAD_DIM)
    return hval * lax.rsqrt(ms + RMS_EPS) * gain * gate


def _block_ones():
    r = _head_of((GW, GW), 0)
    c = _head_of((GW, GW), 1)
    return r == c


def _split3(x):
    hi = x.astype(BF16).astype(F32)
    r = x - hi
    mid = r.astype(BF16).astype(F32)
    lo = (r - mid).astype(BF16).astype(F32)
    return hi, mid, lo


def _mlstm_body(q_ref, k_ref, v_ref, o_ref, g_ref, gain_ref, y_ref, ct_ref, n_ref, m_ref, g_ref2, c_ref2, cm_ref2):
    s_len = q_ref.shape[1]
    L = MLSTM_CHUNK
    iota = lambda shape, axis: lax.broadcasted_iota(jnp.int32, shape, axis)
    bd = _block_ones()
    bones = jnp.where(bd, 1.0, 0.0).astype(BF16)
    hid_l = _head_of((L, GW), 1)
    hmask = [hid_l == h for h in range(HEADS)]
    tril = iota((L, L), 0) >= iota((L, L), 1)
    row8 = iota((8, L), 0)
    row16 = iota((16, L), 0)
    hid1 = _head_of((1, GW), 1)
    gain = gain_ref[...]
    ones_bd = jnp.where(_head_of((HEADS * L, GW), 1) == iota((HEADS * L, GW), 0) // L, 1.0, 0.0).astype(BF16)
    csel = iota((128, 3 * GW), 1)
    esel = jnp.where(iota((128, 3 * GW), 0) == 16 + 4 * (csel // GW) + (csel % GW) // HEAD_DIM, 1.0, 0.0).astype(BF16)
    ones8 = jnp.ones((8, L), BF16)

    ct_ref[...] = jnp.zeros_like(ct_ref)
    n_ref[...] = jnp.zeros_like(n_ref)
    m_ref[...] = jnp.zeros_like(m_ref)

    for bb in range(q_ref.shape[0]):
        gt = jnp.concatenate([jnp.transpose(g_ref[bb, c * L:(c + 1) * L, :])[0:8] for c in range(s_len // L)], axis=1)
        g = _scan_lanes(jax.nn.log_sigmoid(pltpu.roll(gt, 4, 0)), jnp.add, L)
        cc = gt - g
        g_ref2[bb] = g
        c_ref2[bb] = cc
        cm_ref2[bb] = _scan_lanes(cc, jnp.maximum, L)

    def prep(bb, r0):
        q = q_ref[bb, pl.ds(r0, L), :]
        k = (k_ref[bb, pl.ds(r0, L), :].astype(F32) * (HEAD_DIM ** -0.5)).astype(BF16)
        v = v_ref[bb, pl.ds(r0, L), :]
        g = g_ref2[bb, :, pl.ds(r0, L)]
        cc = c_ref2[bb, :, pl.ds(r0, L)]
        mprev = m_ref[bb][:, 0:1]
        pp = jnp.maximum(mprev, cm_ref2[bb, :, pl.ds(r0, L)])
        ein = jnp.exp(mprev - pp)
        p_last = pp[:, L - 1:L]
        ws = jnp.exp(cc - p_last)
        decay = jnp.exp(mprev - p_last)
        m_new = g[:, L - 1:L] + p_last
        emt = jnp.exp(-(g + pp))
        p_hi, p_mid, p_lo = _split3(pp)
        b0 = jnp.where(row8 < 4, p_hi, pltpu.roll(p_mid, 4, 0))
        b1 = jnp.where(row8 < 4, p_lo, jnp.where(row8 < 7, 1.0, 0.0))
        b2 = jnp.where(row8 < 4, ein, pltpu.roll(ws, 4, 0))
        b3 = jnp.where(row8 < 4, emt, 0.0)
        cols = jnp.transpose(jnp.concatenate([b0, b1, b2, b3, jnp.zeros((96, L), F32)], axis=0)).astype(BF16)
        return dict(bb=bb, r0=r0, q=q, k=k, v=v, cpieces=_split3(cc), cols=cols, decay=decay, m_new=m_new, w=[])

    def head_weights(st, h):
        c_hi, c_mid, c_lo = st["cpieces"]
        pick = (row16 == h) | (row16 == 4 + h) | (row16 == 8 + h)
        bh = jnp.where(pick, -1.0, jnp.where(row16 == 12, c_hi[h:h + 1], jnp.where(
            row16 == 13, c_mid[h:h + 1], jnp.where(row16 == 14, c_lo[h:h + 1], 0.0))))
        bh = jnp.concatenate([bh, jnp.zeros((112, L), F32)], axis=0).astype(BF16)
        expo = _dot(st["cols"], bh)
        sc = _nt(jnp.where(hmask[h], st["q"], jnp.zeros_like(st["q"])), st["k"])
        st["w"].append((jnp.exp(jnp.where(tril, expo, NEG_BIG)) * sc).astype(BF16))

    def finish(st):
        bb, r0, q, k, v, cols = st["bb"], st["r0"], st["q"], st["k"], st["v"], st["cols"]
        w_cat = jnp.concatenate(st["w"], axis=1)
        v_bd = jnp.concatenate([jnp.where(hmask[h], v, jnp.zeros_like(v)) for h in range(HEADS)], axis=0)
        num = _dot(w_cat, v_bd)
        den = _dot(w_cat, ones_bd)
        x3 = _dot(cols, esel)
        ein_x, ws_x, emt_x = x3[:, :GW], x3[:, GW:2 * GW], x3[:, 2 * GW:]

        ct = ct_ref[bb]
        nrow = n_ref[bb][0:1]
        nbd = jnp.where(bd, jnp.broadcast_to(nrow, (GW, GW)), 0.0).astype(BF16)
        num = num + ein_x * _nt(q, ct.astype(BF16))
        den = den + ein_x * _nt(q, nbd)
        hval = num / jnp.maximum(jnp.abs(den), emt_x)
        gate = jax.nn.sigmoid(o_ref[bb, pl.ds(r0, L), :].astype(F32))
        y_ref[bb, pl.ds(r0, L), :] = _head_rms_gate(hval, gain, gate, bones).astype(y_ref.dtype)

        decay = st["decay"]
        dec_l = jnp.where(hid1 == 0, decay[0:1], jnp.where(hid1 == 1, decay[1:2], jnp.where(hid1 == 2, decay[2:3], decay[3:4])))
        vw = (v.astype(F32) * ws_x).astype(BF16)
        kw = (k.astype(F32) * ws_x).astype(BF16)
        ct_ref[bb] = ct * dec_l + jnp.where(bd, _tn(vw, k), 0.0)
        n_ref[bb] = jnp.broadcast_to(nrow * dec_l + _dot(ones8, kw)[0:1], n_ref.shape[1:])
        m_ref[bb] = jnp.broadcast_to(st["m_new"], m_ref.shape[1:])

    def step(c, carry):
        r0 = pl.multiple_of(c * L, L)
        states = [prep(bb, r0) for bb in range(q_ref.shape[0])]
        for h in range(HEADS):
            for st in states:
                head_weights(st, h)
        for st in states:
            finish(st)
        return carry

    lax.fori_loop(0, s_len // L, step, 0)


def _mlstm(main3, aux3, gain):
    b, s, _ = main3.shape
    nb = MLSTM_BATCH if b % MLSTM_BATCH == 0 else 1
    col = lambda cb: pl.BlockSpec((nb, s, GW), lambda i: (i, 0, cb))
    return pl.pallas_call(
        _mlstm_body,
        grid=(b // nb,),
        in_specs=[col(CB_MQ), col(CB_MK), col(CB_MV), col(CB_MO),
                  pl.BlockSpec((nb, s, 128), lambda i: (i, 0, 2)),
                  pl.BlockSpec((1, GW), lambda i: (0, 0))],
        out_specs=pl.BlockSpec((nb, s, GW), lambda i: (i, 0, 0)),
        out_shape=jax.ShapeDtypeStruct((b, s, GW), BF16),
        scratch_shapes=[pltpu.VMEM((nb, GW, GW), F32), pltpu.VMEM((nb, 8, GW), F32), pltpu.VMEM((nb, 8, 128), F32)]
        + [pltpu.VMEM((nb, 8, s), F32)] * 3,
        compiler_params=_params("parallel"),
        name="mlstm",
    )(main3, main3, main3, main3, aux3, gain)


def _hgrn_body(lbl_ref, q_ref, i_ref, g_ref, f_ref, gain_ref, y_ref, st_ref, *, layer):
    nbat, s_len = q_ref.shape[0], q_ref.shape[1]
    C = HGRN_CHUNK
    R = nbat * C
    bd = _block_ones()
    bones = jnp.where(bd, 1.0, 0.0).astype(BF16)
    hid_r = _head_of((R, GW), 1)
    hm_b = [jnp.where(hid_r == h, 1.0, 0.0).astype(BF16) for h in range(HEADS)]
    gain = gain_ref[...]

    lg = lbl_ref[...]
    ex = jnp.exp(lg - jnp.max(lg, axis=0, keepdims=True))
    soft = ex / jnp.sum(ex, axis=0, keepdims=True)
    lb = jnp.zeros((1, GW), F32)
    for j in range(1, layer + 1):
        lb = lb + soft[j:j + 1]

    tq = lax.broadcasted_iota(jnp.int32, (C, HEADS * C), 0)
    tk = lax.broadcasted_iota(jnp.int32, (C, HEADS * C), 1) & (C - 1)
    level_mask = {}
    bsz = C
    while bsz >= 2:
        half = bsz // 2
        same = (tq // bsz) == (tk // bsz)
        level_mask[bsz] = same & ((tq % bsz) >= half) & ((tk % bsz) < half)
        bsz = half
    eye = tq == tk
    rowpos = lax.broadcasted_iota(jnp.int32, (R, GW), 0) & (C - 1)
    steps = [1 << j for j in range(C.bit_length() - 1)]
    past_start = {sh: rowpos >= sh for sh in steps}
    upper_half = {sh: (rowpos & sh) != 0 for sh in steps}

    @pl.when(pl.program_id(1) == 0)
    def _():
        st_ref[...] = jnp.zeros_like(st_ref)

    def load(ref, r0):
        return ref[:, pl.ds(r0, C), :].reshape(R, GW)

    def stack_heads(a, bb):
        return jnp.concatenate([(a * hm_b[h])[bb * C:(bb + 1) * C] for h in range(HEADS)], axis=0)

    def step(c, carry):
        r0 = pl.multiple_of(c * C, C)
        f = lb + (1.0 - lb) * jax.nn.sigmoid(load(f_ref, r0))
        lf = jnp.log2(f)
        kk = 1.0 - f
        qq = jax.nn.silu(load(q_ref, r0).astype(F32))
        vv = load(i_ref, r0)
        rows = [slice(bb * C, (bb + 1) * C) for bb in range(nbat)]

        gcum = lf
        for sh in steps:
            gcum = gcum + jnp.where(past_start[sh], pltpu.roll(gcum, sh, 0), 0.0)
        st = [st_ref[bb] for bb in range(nbat)]
        qin = (qq * jnp.exp2(gcum)).astype(BF16)
        out = [_nt(qin[rows[bb]], st[bb].astype(BF16)) for bb in range(nbat)]

        qb, kb = qq.astype(BF16), kk.astype(BF16)
        att = [jnp.where(eye, _nt(qb[rows[bb]], stack_heads(kb, bb)), 0.0) for bb in range(nbat)]
        ge = gcum
        for half in steps:
            qd = (qq * jnp.exp2(jnp.minimum(gcum - pltpu.roll(ge, half, 0), 0.0))).astype(BF16)
            kd = (kk * jnp.exp2(ge - gcum)).astype(BF16)
            att = [jnp.where(level_mask[2 * half], _nt(qd[rows[bb]], stack_heads(kd, bb)), att[bb]) for bb in range(nbat)]
            ge = jnp.where(upper_half[half], ge, pltpu.roll(ge, R - half, 0))
        for bb in range(nbat):
            out[bb] = out[bb] + _dot(att[bb].astype(BF16), stack_heads(vv, bb))

        gate = jax.nn.silu(load(g_ref, r0).astype(F32))
        y = _head_rms_gate(jnp.concatenate(out, axis=0), gain, gate, bones).astype(y_ref.dtype)
        y_ref[:, pl.ds(r0, C), :] = y.reshape(nbat, C, GW)

        for bb in range(nbat):
            g_last = gcum[bb * C + C - 1:bb * C + C, :]
            kdec = (kk[rows[bb]] * jnp.exp2(g_last - gcum[rows[bb]])).astype(BF16)
            st_ref[bb] = st[bb] * jnp.exp2(g_last) + jnp.where(bd, _tn(vv[rows[bb]], kdec), 0.0)
        return carry

    lax.fori_loop(0, s_len // C, step, 0)


def _hgrn(main3, aux3, lb_logits, gain, layer):
    b, s, _ = main3.shape
    nbat = HGRN_BATCH if b % HGRN_BATCH == 0 else 1
    sblk = HGRN_SEQ if s % HGRN_SEQ == 0 else s
    col = lambda cb: pl.BlockSpec((nbat, sblk, GW), lambda i, j: (i, j, cb))
    return pl.pallas_call(
        functools.partial(_hgrn_body, layer=layer),
        grid=(b // nbat, s // sblk),
        in_specs=[pl.BlockSpec(lb_logits.shape, lambda i, j: (0, 0)),
                  col(CB_HQ), col(CB_HI), col(CB_HG),
                  pl.BlockSpec((nbat, sblk, GW), lambda i, j: (i, j, 0)),
                  pl.BlockSpec((1, GW), lambda i, j: (0, 0))],
        out_specs=pl.BlockSpec((nbat, sblk, GW), lambda i, j: (i, j, 0)),
        out_shape=jax.ShapeDtypeStruct((b, s, GW), BF16),
        scratch_shapes=[pltpu.VMEM((nbat, GW, GW), F32)],
        compiler_params=_params("parallel", "arbitrary"),
        name="hgrn2",
    )(lb_logits, main3, main3, main3, aux3, gain)


def _bucket_tables():
    def bucket(dist):
        n = np.maximum(dist, 0)
        exact = REL_BUCKETS // 2
        nf = np.maximum(n, 1).astype(np.float32)
        large = exact + (np.log(nf / np.float32(exact)) / np.float32(math.log(REL_MAX_DIST / exact))
                         * np.float32(REL_BUCKETS - exact)).astype(np.int32)
        large = np.minimum(large, REL_BUCKETS - 1)
        return np.where(n < exact, n, large).astype(np.int32)
    tk = np.arange(MOBA_BLOCK)[:, None]
    tq = np.arange(MOBA_BLOCK)[None, :]
    own = np.where(tq - tk >= 0, bucket(tq - tk), REL_BUCKETS)
    adj = bucket(MOBA_BLOCK + tq - tk)
    far = int(bucket(np.array([2 * MOBA_BLOCK]))[0])
    return np.stack([own, adj]).astype(np.int32), far


def _bias_body(rb_ref, bk_ref, o_ref):
    bk = bk_ref[...]
    bs = bk.shape[-1]
    for h in range(HEADS):
        acc = jnp.full(bk.shape, NEG_BIG, F32)
        for j in range(REL_BUCKETS):
            acc = jnp.where(bk == j, rb_ref[j, h] * LOG2E, acc)
        o_ref[:, :, h * bs:(h + 1) * bs] = acc


def _bias_tiles(rel_bias, buckets):
    two, bk, bq = buckets.shape
    return pl.pallas_call(
        _bias_body,
        in_specs=[pl.BlockSpec(memory_space=pltpu.SMEM), pl.BlockSpec(buckets.shape, lambda: (0, 0, 0))],
        out_specs=pl.BlockSpec((two, bk, HEADS * bq), lambda: (0, 0, 0)),
        out_shape=jax.ShapeDtypeStruct((two, bk, HEADS * bq), F32),
        name="moba_bias",
    )(rel_bias, jnp.asarray(buckets))


VT_ROWS = HEAD_DIM + 16


def _moba_body(rb_ref, q_ref, k_ref, v_ref, bias_ref, y_ref, vt_ref, km_ref, qbd_ref, *, far_bucket):
    jp = pl.program_id(1)
    BS = MOBA_BLOCK
    nb = k_ref.shape[1] // BS
    n_sel = min(MOBA_TOPK, nb - 1)
    scale = HEAD_DIM ** -0.5

    @pl.when(jp == 0)
    def _():
        hid = _head_of((1, GW), 1)
        vt_ref[:, HEAD_DIM:, :] = jnp.ones((HEADS, VT_ROWS - HEAD_DIM, nb * BS), BF16)
        for n in range(nb):
            cols = slice(n * BS, (n + 1) * BS)
            vt = jnp.transpose(v_ref[0, cols, :].astype(F32)).astype(BF16)
            km = jnp.mean(k_ref[0, cols, :].astype(F32), axis=0, keepdims=True)
            for h in range(HEADS):
                vt_ref[h, 0:HEAD_DIM, cols] = vt[h * HEAD_DIM:(h + 1) * HEAD_DIM]
                km_ref[h * nb + n:h * nb + n + 1, :] = jnp.where(hid == h, km, 0.0).astype(BF16)

    hid_q = _head_of((BS, GW), 1)
    blk = lax.broadcasted_iota(jnp.int32, (nb, BS), 0)
    sels = []
    for t in range(MOBA_TILES):
        i = MOBA_TILES * jp + t
        q = q_ref[0, t * BS:(t + 1) * BS, :]
        qs = (q.astype(F32) * (scale * LOG2E)).astype(BF16)
        for h in range(HEADS):
            qbd_ref[t, h * BS:(h + 1) * BS, :] = jnp.where(hid_q == h, qs, jnp.zeros_like(qs))
        gate = _nt(km_ref[...], q)
        sel = []
        for h in range(HEADS):
            g = jnp.where(blk < i, gate[h * nb:(h + 1) * nb], NEG_BIG)
            rank = jnp.zeros((nb, BS), jnp.int32)
            for mrow in range(nb):
                gm = g[mrow:mrow + 1]
                ahead = (gm > g) | ((gm == g) & (mrow < blk))
                rank = rank + jnp.where(ahead, 1, 0)
            chosen = (rank < n_sel) & (blk < i)
            far = jnp.where(blk < i - 1, rb_ref[far_bucket, h] * LOG2E, 0.0)
            sel.append(jnp.where(chosen, far, NEG_BIG))
        sels.append(jnp.concatenate(sel, axis=1))

    def attend(pair):
        lanes = [(t, h) for t in range(MOBA_TILES) for h in range(HEADS)]
        past = [MOBA_TILES * pair + t for t in range(MOBA_TILES)]
        s_all = {(t, h): _nt(k_ref[0, 0:(past[t] + 1) * BS, :], qbd_ref[t, h * BS:(h + 1) * BS, :]) for t, h in lanes}
        m = dict.fromkeys(lanes)
        acc = dict.fromkeys(lanes)
        for n in range(max(past) + 1):
            live = [(t, h) for t, h in lanes if n <= past[t]]
            p, alpha = {}, {}
            for t, h in live:
                hs = slice(h * BS, (h + 1) * BS)
                s = s_all[t, h][n * BS:(n + 1) * BS]
                if n == past[t]:
                    s, r = s + bias_ref[0, :, hs], None
                elif n == past[t] - 1:
                    s, r = s + bias_ref[1, :, hs], sels[t][n:n + 1, hs]
                else:
                    r = sels[t][n:n + 1, hs]
                bm = jnp.max(s, axis=0, keepdims=True)
                bm = bm if r is None else bm + r
                m_new = bm if m[t, h] is None else jnp.maximum(m[t, h], bm)
                p[t, h] = jnp.exp2(s - (m_new if r is None else m_new - r)).astype(BF16)
                alpha[t, h] = None if m[t, h] is None else jnp.exp2(m[t, h] - m_new)
                m[t, h] = m_new
            for t, h in live:
                pv = _dot(vt_ref[h, :, n * BS:(n + 1) * BS], p[t, h])
                acc[t, h] = pv if alpha[t, h] is None else alpha[t, h] * acc[t, h] + pv
        for t in range(MOBA_TILES):
            outs = [acc[t, h][0:HEAD_DIM] * (1.0 / acc[t, h][HEAD_DIM:HEAD_DIM + 1]) for h in range(HEADS)]
            y_ref[0, t * BS:(t + 1) * BS, :] = jnp.transpose(jnp.concatenate(outs, axis=0)).astype(y_ref.dtype)

    for pair in range(nb // MOBA_TILES):
        pl.when(jp == pair)(functools.partial(attend, pair))


def _moba(main3, rel_bias, bias_tiles, far_bucket):
    b, s, _ = main3.shape
    nb = s // MOBA_BLOCK
    rows = MOBA_TILES * MOBA_BLOCK
    return pl.pallas_call(
        functools.partial(_moba_body, far_bucket=far_bucket),
        grid=(b, nb // MOBA_TILES),
        in_specs=[pl.BlockSpec(memory_space=pltpu.SMEM),
                  pl.BlockSpec((1, rows, GW), lambda bi, i: (bi, i, CB_AQ)),
                  pl.BlockSpec((1, s, GW), lambda bi, i: (bi, 0, CB_AK)),
                  pl.BlockSpec((1, s, GW), lambda bi, i: (bi, 0, CB_AV)),
                  pl.BlockSpec(bias_tiles.shape, lambda bi, i: (0, 0, 0), pipeline_mode=pl.Buffered(1))],
        out_specs=pl.BlockSpec((1, rows, GW), lambda bi, i: (bi, i, 0)),
        out_shape=jax.ShapeDtypeStruct((b, s, GW), BF16),
        scratch_shapes=[pltpu.VMEM((HEADS, VT_ROWS, s), BF16),
                        pltpu.VMEM((HEADS * nb, GW), BF16),
                        pltpu.VMEM((MOBA_TILES, HEADS * MOBA_BLOCK, GW), BF16)],
        compiler_params=_params("parallel", "arbitrary"),
        name="moba",
    )(rel_bias, main3, main3, main3, bias_tiles)


def _memkv_body(mem_ref, g_ref, wk_ref, wv_ref, k_ref, v_ref):
    mn = _rms(mem_ref[0], g_ref[...]).astype(BF16)
    k_ref[0] = _dot(mn, wk_ref[...]).astype(BF16)
    v_ref[0] = _dot(mn, wv_ref[...]).astype(BF16)


def _memkv(mem, gains, wk, wv):
    b, nm, d = mem.shape
    depth, _, cw = wk.shape
    of_layer = lambda a: pl.BlockSpec((None,) + a.shape[1:], lambda l, i: (l, 0, 0))
    return pl.pallas_call(
        _memkv_body,
        grid=(depth, b),
        in_specs=[pl.BlockSpec((1, nm, d), lambda l, i: (i, 0, 0)), of_layer(gains), of_layer(wk), of_layer(wv)],
        out_specs=[pl.BlockSpec((1, nm, cw), lambda l, i: (l * b + i, 0, 0))] * 2,
        out_shape=[jax.ShapeDtypeStruct((depth * b, nm, cw), BF16)] * 2,
        compiler_params=_params("parallel", "parallel"),
        name="memkv",
    )(mem, gains, wk, wv)


def _mixout_cross_body(ym_ref, yc_ref, ya_ref, yh_ref, wout_ref, gmix_ref, x_ref,
                       gpre_ref, wq_ref, k_ref, v_ref, wo_ref, gpost_ref, o_ref):
    h = (_dot(ym_ref[0], wout_ref[0 * GW:1 * GW, :]) + _dot(yc_ref[0], wout_ref[1 * GW:2 * GW, :])
         + _dot(ya_ref[0], wout_ref[2 * GW:3 * GW, :]) + _dot(yh_ref[0], wout_ref[3 * GW:4 * GW, :]))
    x = x_ref[0] + _rms(h, gmix_ref[...])
    xn = _rms(x, gpre_ref[...]).astype(BF16)
    q = _dot(xn, wq_ref[...]).astype(BF16)
    k = k_ref[0]
    v = v_ref[0]
    outs = []
    for hh in range(CROSS_HEADS):
        sl = slice(hh * CROSS_HEAD_DIM, (hh + 1) * CROSS_HEAD_DIM)
        s = _nt(q[:, sl], k[:, sl]) * (CROSS_HEAD_DIM ** -0.5)
        p = jnp.exp(s - jnp.max(s, axis=-1, keepdims=True))
        inv = 1.0 / jnp.sum(p, axis=-1, keepdims=True)
        outs.append((_dot(p.astype(BF16), v[:, sl]) * inv).astype(BF16))
    o = jnp.concatenate(outs, axis=-1)
    o_ref[0] = x + _rms(_dot(o, wo_ref[...]), gpost_ref[...])


def _mixout_cross(ys, wout, gmix, x3, gpre, wq, km, vm, wo, gpost, layer, tq):
    b, s, d = x3.shape
    nm, cw = km.shape[1:]
    const = lambda bi, i: (0, 0)
    of_layer = lambda a: pl.BlockSpec((None,) + a.shape[1:], lambda bi, i: (layer, 0, 0))
    tile = lambda width: pl.BlockSpec((1, tq, width), lambda bi, i: (bi, i, 0))
    mem = pl.BlockSpec((1, nm, cw), lambda bi, i: (layer * b + bi, 0, 0))
    return pl.pallas_call(
        _mixout_cross_body,
        grid=(b, s // tq),
        in_specs=[tile(GW), tile(GW), tile(GW), tile(GW),
                  of_layer(wout), pl.BlockSpec((1, d), const), tile(d),
                  pl.BlockSpec((1, d), const), of_layer(wq), mem, mem,
                  of_layer(wo), pl.BlockSpec((1, d), const)],
        out_specs=tile(d),
        out_shape=jax.ShapeDtypeStruct((b, s, d), F32),
        compiler_params=_params("parallel", "parallel"),
        name="mixout_cross",
    )(*ys, wout, gmix, x3, gpre, wq, km, vm, wo, gpost)


def _ffn_body(x_ref, xh_ref, gpre_ref, wg_ref, wu_ref, cw_ref, cb_ref, wd_ref, gpost_ref, o_ref,
              xn_ref, act_ref, *, seq_len):
    i = pl.program_id(0)
    tm = x_ref.shape[0]
    g = gpre_ref[...]
    x = x_ref[...]
    xn_ref[8:, :] = _rms(x, g).astype(BF16)
    starts_seq = (i * tm) % seq_len == 0
    xn_ref[0:8, :] = jnp.where(starts_seq, 0.0, _rms(xh_ref[...], g)).astype(BF16)
    xe = xn_ref[...]
    for c in range(0, act_ref.shape[1], FFN_CHUNK):
        cs = slice(c, c + FFN_CHUNK)
        ge = _dot(xe, wg_ref[:, cs])
        up = _dot(xe[8:], wu_ref[:, cs])
        cw = cw_ref[:, cs]
        gate = cw[0:1] * ge[6:6 + tm] + cw[1:2] * ge[7:7 + tm] + cw[2:3] * ge[8:8 + tm] + cb_ref[:, cs]
        act_ref[:, cs] = (jax.nn.silu(gate) * up).astype(BF16)
    o_ref[...] = x + _rms(_dot(act_ref[...], wd_ref[...]), gpost_ref[...])


def _ffn(x2, gpre, w_in, cw, cb, wd, gpost, layer, seq_len, tm):
    m, d = x2.shape
    f = wd.shape[1]
    const = lambda i: (0, 0)
    resident = lambda shape, col=0: pl.BlockSpec((None,) + shape, lambda i: (layer, 0, col), pipeline_mode=pl.Buffered(1))
    return pl.pallas_call(
        functools.partial(_ffn_body, seq_len=seq_len),
        grid=(m // tm,),
        in_specs=[pl.BlockSpec((tm, d), lambda i: (i, 0)),
                  pl.BlockSpec((8, d), lambda i: (jnp.maximum(i * (tm // 8) - 1, 0), 0)),
                  pl.BlockSpec((1, d), const),
                  resident((d, f), 0), resident((d, f), 1),
                  pl.BlockSpec((None, 3, f), lambda i: (layer, 0, 0)), pl.BlockSpec((1, f), const),
                  resident((f, d)),
                  pl.BlockSpec((1, d), const)],
        out_specs=pl.BlockSpec((tm, d), lambda i: (i, 0)),
        out_shape=jax.ShapeDtypeStruct((m, d), F32),
        scratch_shapes=[pltpu.VMEM((tm + 8, d), BF16), pltpu.VMEM((tm, f), BF16)],
        compiler_params=_params("parallel"),
        name="ffn",
    )(x2, x2, gpre, w_in, w_in, cw, cb, wd, gpost)


def _row_tile(m, pref):
    t = pref
    while m % t:
        t //= 2
    return t


def kernel(x, mem, w_in, b_in, mlstm_norm, sconv_w, rel_bias, hgrn_lb_logits, hgrn_norm, w_mix_out, norm_mix_pre, norm_mix_post, mem_norm, w_cq, w_ck, w_cv, w_co, norm_cross_pre, norm_cross_post, w_ffn_in, ffn_conv_w, ffn_conv_b, w_ffn_out, norm_ffn_pre, norm_ffn_post):
    b, s, d = x.shape
    depth = w_in.shape[0]
    d_ff = w_ffn_out.shape[1]
    assert s % (MOBA_TILES * MOBA_BLOCK) == 0 and s % MLSTM_CHUNK == 0 and s % HGRN_CHUNK == 0
    m = b * s
    tm = _row_tile(m, 512)

    gate0 = 4 * GW
    hq0 = gate0 + 2 * HEADS + 6 * GW
    main_cols = lambda a: jnp.concatenate([a[..., :gate0], a[..., gate0 + 2 * HEADS:hq0 + GW], a[..., hq0 + 2 * GW:]], axis=-1)
    aux_cols = lambda a: jnp.concatenate(
        [a[..., hq0 + GW:hq0 + 2 * GW], a[..., gate0:gate0 + 2 * HEADS],
         jnp.zeros(a.shape[:-1] + (N_AUX - GW - 2 * HEADS,), a.dtype)], axis=-1)
    w_in_b = w_in.astype(BF16)
    wm_all = main_cols(w_in_b)
    wa_all = aux_cols(w_in_b)
    bm_all = main_cols(b_in)[:, None, :]
    ba_all = aux_cols(b_in)[:, None, :]

    w_out = w_mix_out.astype(BF16)
    wq, wk, wv, wo = (w.astype(BF16) for w in (w_cq, w_ck, w_cv, w_co))
    w_ffn = w_ffn_in.astype(BF16)
    w_down = w_ffn_out.astype(BF16)
    row = lambda a, l: a[l][None, :]

    buckets, far_bucket = _bucket_tables()
    bias_tiles = _bias_tiles(rel_bias, buckets)
    assert d_ff % FFN_CHUNK == 0

    km, vm = _memkv(mem, mem_norm[:, None, :], wk, wv)

    x2 = x.reshape(m, d)
    for l in range(depth):
        main, aux, y_c = _inproj(x2, row(norm_mix_pre, l), wm_all, bm_all, wa_all, ba_all, sconv_w, l, s, tm)
        y_c = y_c.reshape(b, s, GW)
        main3 = main.reshape(b, s, N_MAIN)
        aux3 = aux.reshape(b, s, N_AUX)
        y_m = _mlstm(main3, aux3, row(mlstm_norm, l))
        y_a = _moba(main3, rel_bias, bias_tiles, far_bucket)
        y_h = _hgrn(main3, aux3, hgrn_lb_logits, row(hgrn_norm, l), l)
        x2 = _mixout_cross((y_m, y_c, y_a, y_h), w_out, row(norm_mix_post, l), x2.reshape(b, s, d),
                           row(norm_cross_pre, l), wq, km, vm, wo, row(norm_cross_post, l),
                           l, _row_tile(s, 1024)).reshape(m, d)

        x2 = _ffn(x2, row(norm_ffn_pre, l), w_ffn, ffn_conv_w, row(ffn_conv_b, l),
                  w_down, row(norm_ffn_post, l), l, s, _row_tile(m, FFN_ROWS))
    return x2.reshape(b, s, d)
```
